```python
import jax, jax.numpy as jnp
from jax import lax
import numpy as np

D_MODEL = 1024
BATCH = 16
SEQ = 2048
DEPTH = 4

HEAD_DIM = 64
N_HEADS = D_MODEL // 256
BRANCH_WIDTH = N_HEADS * HEAD_DIM
N_BRANCHES = 4
ROT_DIM = HEAD_DIM // 4
ROPE_THETA = 500000.0
Q_BLOCK = 128
DILATED_PAIRS = ((128, 1), (512, 4), (2048, 16))
WIN_BLOCK = 128
MOBA_BLOCK = 256
MOBA_TOPK = 3
MOBA_Q_CHUNK = 32
MLA_Q_RANK = D_MODEL // 4
MLA_KV_RANK = D_MODEL // 8
MLA_NOPE_DIM = 64
MLA_ROPE_DIM = 32
MLA_V_DIM = 64
D_FF = 4 * D_MODEL
NORM_EPS = 1e-6
FORGET_BIAS_INIT = 3.0
IN_SIZES = (3 * BRANCH_WIDTH, N_HEADS, 3 * BRANCH_WIDTH, 3 * BRANCH_WIDTH,
            MLA_Q_RANK, MLA_KV_RANK, MLA_ROPE_DIM, N_BRANCHES * D_MODEL)
D_IN = sum(IN_SIZES)

kernel_name = "hybrid_fox_dilated_moba_mla_block"


def rms_norm(x, g):
    xf = x.astype(jnp.float32)
    y = xf * lax.rsqrt(jnp.mean(jnp.square(xf), axis=-1, keepdims=True) + NORM_EPS)
    return (y * g.astype(jnp.float32)).astype(x.dtype)


def rope_tables(seq, dim):
    inv_freq = 1.0 / (ROPE_THETA ** (jnp.arange(0, dim, 2, dtype=jnp.float32) / dim))
    ang = jnp.arange(seq, dtype=jnp.float32)[:, None] * inv_freq[None, :]
    return jnp.cos(ang), jnp.sin(ang)


def apply_rope(x, cos, sin):
    xf = x.astype(jnp.float32)
    x1, x2 = jnp.split(xf, 2, axis=-1)
    return jnp.concatenate([x1 * cos - x2 * sin, x2 * cos + x1 * sin], axis=-1).astype(x.dtype)


def partial_rope(x, cos, sin):
    return jnp.concatenate([apply_rope(x[..., :ROT_DIM], cos, sin), x[..., ROT_DIM:]], axis=-1)


def to_heads(t):
    b, s, c = t.shape
    return t.reshape(b, s, N_HEADS, c // N_HEADS).transpose(0, 2, 1, 3)


def from_heads(t):
    b, h, s, d = t.shape
    return t.transpose(0, 2, 1, 3).reshape(b, s, h * d)


def blocked_causal_attention(q, k, v, scale, cum=None):
    b, h, s, dk = q.shape
    nq = s // Q_BLOCK
    qb = q.reshape(b, h, nq, Q_BLOCK, dk).transpose(2, 0, 1, 3, 4)
    kpos = jnp.arange(s)
    xs = (jnp.arange(nq), qb)
    if cum is not None:
        xs = xs + (cum.reshape(b, h, nq, Q_BLOCK).transpose(2, 0, 1, 3),)

    def one_block(args):
        qpos = args[0] * Q_BLOCK + jnp.arange(Q_BLOCK)
        sc = jnp.einsum('bhqd,bhkd->bhqk', args[1], k).astype(jnp.float32) * scale
        if cum is not None:
            sc = sc + args[2][..., :, None] - cum[:, :, None, :]
        sc = jnp.where(kpos[None, :] <= qpos[:, None], sc, -jnp.inf)
        p = jax.nn.softmax(sc, axis=-1).astype(v.dtype)
        return jnp.einsum('bhqk,bhkd->bhqd', p, v)

    out = lax.map(one_block, xs)
    return out.transpose(1, 2, 0, 3, 4).reshape(b, h, s, v.shape[-1])


def banded_window_attention(q, k, v, window, scale):
    n, length, dh = q.shape
    lp = -(-length // WIN_BLOCK) * WIN_BLOCK
    pad = ((0, 0), (0, lp - length), (0, 0))
    nb = lp // WIN_BLOCK
    qb, kb, vb = (jnp.pad(t, pad).reshape(n, nb, WIN_BLOCK, dh) for t in (q, k, v))

    def with_prev(t):
        prev = jnp.pad(t, ((0, 0), (1, 0), (0, 0), (0, 0)))[:, :-1]
        return jnp.concatenate([prev, t], axis=2)

    k2, v2 = with_prev(kb), with_prev(vb)
    sc = jnp.einsum('nbqd,nbkd->nbqk', qb, k2).astype(jnp.float32) * scale
    qi = jnp.arange(WIN_BLOCK)[:, None] + WIN_BLOCK
    kj = jnp.arange(2 * WIN_BLOCK)[None, :]
    dist = qi - kj
    band = (dist >= 0) & (dist <= window)
    real = (jnp.arange(nb) > 0)[:, None, None] | (kj >= WIN_BLOCK)[None]
    sc = jnp.where(band[None] & real, sc, -jnp.inf)
    lse = jax.nn.logsumexp(sc, axis=-1)
    p = jnp.exp(sc - lse[..., None]).astype(v.dtype)
    out = jnp.einsum('nbqk,nbkd->nbqd', p, v2).reshape(n, lp, dh)[:, :length]
    return out, lse.reshape(n, lp)[:, :length]


def dilated_attention(q, k, v):
    b, h, s, dh = q.shape
    scale = dh ** -0.5
    outs, lses = [], []
    for window, dil in DILATED_PAIRS:
        sub = s // dil

        def to_sub(t):
            return t.reshape(b, h, sub, dil, dh).transpose(0, 1, 3, 2, 4).reshape(b * h * dil, sub, dh)

        o, l = banded_window_attention(to_sub(q), to_sub(k), to_sub(v), window // dil, scale)
        outs.append(o.reshape(b, h, dil, sub, dh).transpose(0, 1, 3, 2, 4).reshape(b, h, s, dh))
        lses.append(l.reshape(b, h, dil, sub).transpose(0, 1, 3, 2).reshape(b, h, s))
    wts = jax.nn.softmax(jnp.stack(lses, axis=0), axis=0)
    return jnp.sum(wts[..., None].astype(v.dtype) * jnp.stack(outs, axis=0), axis=0)


def moba_attention(q, k, v):
    b, h, s, dh = q.shape
    scale = dh ** -0.5
    sp = -(-s // MOBA_BLOCK) * MOBA_BLOCK
    padw = ((0, 0), (0, 0), (0, sp - s), (0, 0))
    q, k, v = jnp.pad(q, padw), jnp.pad(k, padw), jnp.pad(v, padw)
    nblk = sp // MOBA_BLOCK
    n_sel = min(MOBA_TOPK, nblk - 1)
    kb = k.reshape(b, h, nblk, MOBA_BLOCK, dh)
    vb = v.reshape(b, h, nblk, MOBA_BLOCK, dh)
    nc = sp // MOBA_Q_CHUNK
    xs = (jnp.arange(nc), q.reshape(b, h, nc, MOBA_Q_CHUNK, dh).transpose(2, 0, 1, 3, 4))
    if n_sel > 0:
        kmean = jnp.mean(kb.astype(jnp.float32), axis=3)
        gate = jnp.einsum('bhsd,bhnd->bhsn', q.astype(jnp.float32), kmean)
        past = jnp.arange(nblk)[None, :] < (jnp.arange(sp) // MOBA_BLOCK)[:, None]
        _, sel = lax.top_k(jnp.where(past, gate, -jnp.inf), n_sel)
        xs = xs + (sel.reshape(b, h, nc, MOBA_Q_CHUNK, n_sel).transpose(2, 0, 1, 3, 4),)
    bi = jnp.arange(b)[:, None, None, None]
    hi = jnp.arange(h)[None, :, None, None]

    def one_chunk(args):
        i, qi = args[0], args[1]
        qpos = i * MOBA_Q_CHUNK + jnp.arange(MOBA_Q_CHUNK)
        own = (i * MOBA_Q_CHUNK) // MOBA_BLOCK
        k_own = lax.dynamic_index_in_dim(kb, own, axis=2, keepdims=False)
        v_own = lax.dynamic_index_in_dim(vb, own, axis=2, keepdims=False)
        kpos = own * MOBA_BLOCK + jnp.arange(MOBA_BLOCK)
        s_own = jnp.einsum('bhqd,bhkd->bhqk', qi, k_own).astype(jnp.float32) * scale
        s_own = jnp.where(kpos[None, :] <= qpos[:, None], s_own, -jnp.inf)
        if n_sel == 0:
            p = jax.nn.softmax(s_own, axis=-1).astype(v.dtype)
            return jnp.einsum('bhqk,bhkd->bhqd', p, v_own)
        seli = args[2]
        k_sel = kb[bi, hi, seli]
        v_sel = vb[bi, hi, seli]
        s_sel = jnp.einsum('bhqd,bhqnkd->bhqnk', qi, k_sel).astype(jnp.float32) * scale
        valid = jnp.arange(n_sel)[None, :] < (qpos // MOBA_BLOCK)[:, None]
        s_sel = jnp.where(valid[:, :, None], s_sel, -jnp.inf)
        s_sel = s_sel.reshape(b, h, MOBA_Q_CHUNK, n_sel * MOBA_BLOCK)
        p = jax.nn.softmax(jnp.concatenate([s_sel, s_own], axis=-1), axis=-1)
        p_sel = p[..., :n_sel * MOBA_BLOCK].reshape(b, h, MOBA_Q_CHUNK, n_sel, MOBA_BLOCK).astype(v.dtype)
        p_own = p[..., n_sel * MOBA_BLOCK:].astype(v.dtype)
        return (jnp.einsum('bhqnk,bhqnkd->bhqd', p_sel, v_sel)
                + jnp.einsum('bhqk,bhkd->bhqd', p_own, v_own))

    out = lax.map(one_chunk, xs)
    return out.transpose(1, 2, 0, 3, 4).reshape(b, h, sp, dh)[:, :, :s]


def mla_attention(cq_raw, ckv_raw, kr_raw, g_cq, g_ckv, w_uq, w_uk, w_uv, cos_m, sin_m):
    b, s, _ = cq_raw.shape
    c_q = rms_norm(cq_raw, g_cq)
    c_kv = rms_norm(ckv_raw, g_ckv)
    q = to_heads(c_q @ w_uq)
    q = jnp.concatenate([q[..., :MLA_NOPE_DIM], apply_rope(q[..., MLA_NOPE_DIM:], cos_m, sin_m)], axis=-1)
    k_nope = to_heads(c_kv @ w_uk)
    v = to_heads(c_kv @ w_uv)
    k_rope = apply_rope(kr_raw[:, None], cos_m, sin_m)
    k = jnp.concatenate([k_nope, jnp.broadcast_to(k_rope, (b, N_HEADS, s, MLA_ROPE_DIM))], axis=-1)
    return blocked_causal_attention(q, k, v, (MLA_NOPE_DIM + MLA_ROPE_DIM) ** -0.5)


def hybrid_mixer(h, w_in, b_forget, g_cq, g_ckv, w_uq, w_uk, w_uv, w_branch, w_out,
                 cos_p, sin_p, cos_m, sin_m):
    b, s, d = h.shape
    proj = h @ w_in
    offs = np.cumsum(IN_SIZES)[:-1].tolist()
    fox_qkv, fox_f, dil_qkv, moba_qkv, mla_cq, mla_ckv, mla_kr, gates = jnp.split(proj, offs, axis=-1)

    q, k, v = (to_heads(t) for t in jnp.split(fox_qkv, 3, axis=-1))
    log_f = jax.nn.log_sigmoid((fox_f + b_forget).astype(jnp.float32))
    cum = jnp.cumsum(log_f, axis=1).transpose(0, 2, 1)
    y_fox = blocked_causal_attention(q, k, v, HEAD_DIM ** -0.5, cum)

    q, k, v = (to_heads(t) for t in jnp.split(dil_qkv, 3, axis=-1))
    y_dil = dilated_attention(partial_rope(q, cos_p, sin_p), partial_rope(k, cos_p, sin_p), v)

    q, k, v = (to_heads(t) for t in jnp.split(moba_qkv, 3, axis=-1))
    y_moba = moba_attention(partial_rope(q, cos_p, sin_p), partial_rope(k, cos_p, sin_p), v)

    y_mla = mla_attention(mla_cq, mla_ckv, mla_kr, g_cq, g_ckv, w_uq, w_uk, w_uv, cos_m, sin_m)

    ys = jnp.stack([from_heads(y_fox), from_heads(y_dil), from_heads(y_moba), from_heads(y_mla)], axis=2)
    u = jnp.einsum('bsnc,ncd->bsnd', ys, w_branch)
    g = jax.nn.sigmoid(gates.reshape(b, s, N_BRANCHES, d))
    merged = jnp.sum(g * u, axis=2)
    return merged @ w_out


def squared_relu_mlp(h, w_up, w_down):
    return jnp.square(jax.nn.relu(h @ w_up)) @ w_down


def setup_inputs(seed: int = 0) -> dict:
    key = jax.random.key(seed)
    ks = jax.random.split(key, 16)

    def nrm(k, shape, fan_in):
        return jax.random.normal(k, shape, jnp.float32) * fan_in ** -0.5

    def gain(k, shape):
        return 1.0 + 0.05 * jax.random.normal(k, shape, jnp.float32)

    return {
        "x": jax.random.normal(ks[0], (BATCH, SEQ, D_MODEL), jnp.float32),
        "w_in": nrm(ks[1], (DEPTH, D_MODEL, D_IN), D_MODEL),
        "b_forget": FORGET_BIAS_INIT + 0.5 * jax.random.normal(ks[2], (DEPTH, N_HEADS), jnp.float32),
        "g_cq": gain(ks[3], (DEPTH, MLA_Q_RANK)),
        "g_ckv": gain(ks[4], (DEPTH, MLA_KV_RANK)),
        "w_uq": nrm(ks[5], (DEPTH, MLA_Q_RANK, N_HEADS * (MLA_NOPE_DIM + MLA_ROPE_DIM)), MLA_Q_RANK),
        "w_uk": nrm(ks[6], (DEPTH, MLA_KV_RANK, N_HEADS * MLA_NOPE_DIM), MLA_KV_RANK),
        "w_uv": nrm(ks[7], (DEPTH, MLA_KV_RANK, N_HEADS * MLA_V_DIM), MLA_KV_RANK),
        "w_branch": nrm(ks[8], (DEPTH, N_BRANCHES, BRANCH_WIDTH, D_MODEL), BRANCH_WIDTH),
        "w_out": nrm(ks[9], (DEPTH, D_MODEL, D_MODEL), D_MODEL),
        "w_up": nrm(ks[10], (DEPTH, D_MODEL, D_FF), D_MODEL),
        "w_down": nrm(ks[11], (DEPTH, D_FF, D_MODEL), D_FF),
        "g_pre_mix": gain(ks[12], (DEPTH, D_MODEL)),
        "g_post_mix": gain(ks[13], (DEPTH, D_MODEL)),
        "g_pre_mlp": gain(ks[14], (DEPTH, D_MODEL)),
        "g_post_mlp": gain(ks[15], (DEPTH, D_MODEL)),
    }


def reference(x, w_in, b_forget, g_cq, g_ckv, w_uq, w_uk, w_uv, w_branch, w_out, w_up, w_down,
              g_pre_mix, g_post_mix, g_pre_mlp, g_post_mlp):
    s = x.shape[1]
    cos_p, sin_p = rope_tables(s, ROT_DIM)
    cos_m, sin_m = rope_tables(s, MLA_ROPE_DIM)
    for l in range(DEPTH):
        h = rms_norm(x, g_pre_mix[l])
        mix = hybrid_mixer(h, w_in[l], b_forget[l], g_cq[l], g_ckv[l], w_uq[l], w_uk[l], w_uv[l],
                           w_branch[l], w_out[l], cos_p, sin_p, cos_m, sin_m)
        x = x + rms_norm(mix, g_post_mix[l])
        h = rms_norm(x, g_pre_mlp[l])
        x = x + rms_norm(squared_relu_mlp(h, w_up[l], w_down[l]), g_post_mlp[l])
    return x
```

```python
import functools

import numpy as np
import jax
import jax.numpy as jnp
from jax import lax
from jax.experimental import pallas as pl
from jax.experimental.pallas import tpu as pltpu

D_MODEL = 1024
SEQ = 2048
HEAD_DIM = 64
N_HEADS = 4
BRANCH_WIDTH = N_HEADS * HEAD_DIM
HEAD_PAD = 128
QK_PAD = N_HEADS * HEAD_PAD
AUG = HEAD_DIM
ROT_DIM = HEAD_DIM // 4
ROPE_THETA = 500000.0
DILATED_PAIRS = ((128, 1), (512, 4), (2048, 16))
MOBA_BLOCK = 256
MOBA_TOPK = 3
N_MOBA_BLOCKS = SEQ // MOBA_BLOCK
MLA_Q_RANK = 256
MLA_KV_RANK = 128
MLA_NOPE_DIM = 64
MLA_ROPE_DIM = 32
D_FF = 4 * D_MODEL
NORM_EPS = 1e-6
NEG = -1e30

TOKEN_TILE = 512
ATTN_TILE = 256
FF_CHUNK = 512
VMEM_LIMIT = 56 * 1024 * 1024

F32 = jnp.float32
BF16 = jnp.bfloat16


def _dot(a, b):
    return jnp.dot(a, b, preferred_element_type=F32)


def _dot_nt(a, b):
    return lax.dot_general(a, b, (((1,), (1,)), ((), ())), preferred_element_type=F32)


def _rms(x, g):
    return x * lax.rsqrt(jnp.mean(x * x, axis=-1, keepdims=True) + NORM_EPS) * g


def _rope(y, a, bm, bp, half):
    n = y.shape[-1]
    return y * a + pltpu.roll(y, n - half, 1) * bm + pltpu.roll(y, half, 1) * bp


def _split3(x):
    hi = x.astype(BF16)
    r = x - hi.astype(F32)
    mid = r.astype(BF16)
    lo = (r - mid.astype(F32)).astype(BF16)
    return hi, mid, lo


def _rope_tables():
    def tables(dim):
        inv_freq = 1.0 / (ROPE_THETA ** (jnp.arange(0, dim, 2, dtype=F32) / dim))
        ang = jnp.arange(SEQ, dtype=F32)[:, None] * inv_freq[None, :]
        return jnp.cos(ang), jnp.sin(ang)

    def expand(cos, sin, width, period, start, half):
        pos = np.arange(width) % period - start
        first = (pos >= 0) & (pos < half)
        second = (pos >= half) & (pos < 2 * half)
        idx = np.where(first, pos, np.where(second, pos - half, 0))
        a = jnp.where((first | second)[None, :], cos[:, idx], 1.0)
        bm = jnp.where(first[None, :], -sin[:, idx], 0.0)
        bp = jnp.where(second[None, :], sin[:, idx], 0.0)
        return a, bm, bp

    cos_p, sin_p = tables(ROT_DIM)
    cos_m, sin_m = tables(MLA_ROPE_DIM)
    partial = expand(cos_p, sin_p, BRANCH_WIDTH, HEAD_DIM, 0, ROT_DIM // 2)
    mla = expand(cos_m, sin_m, HEAD_PAD, HEAD_PAD, MLA_NOPE_DIM, MLA_ROPE_DIM // 2)
    return partial, mla


def _head_placement(scale):
    p = np.zeros((BRANCH_WIDTH, QK_PAD), np.float32)
    for h in range(N_HEADS):
        for d in range(HEAD_DIM):
            p[h * HEAD_DIM + d, h * HEAD_PAD + d] = scale
    return jnp.asarray(p, BF16)


def _lane_placement(per_head, offset):
    p = np.zeros((HEAD_PAD, QK_PAD), np.float32)
    for h in range(N_HEADS):
        for n in range(per_head):
            p[per_head * h + n, h * HEAD_PAD + AUG + offset + n] = 1.0
    return jnp.asarray(p, BF16)


def _aug_ones(lo, hi):
    c = np.zeros((1, QK_PAD), np.float32)
    for h in range(N_HEADS):
        c[0, h * HEAD_PAD + AUG + lo:h * HEAD_PAD + AUG + hi] = 1.0
    return jnp.asarray(c)


def _dilated_weights():
    r = np.arange(ATTN_TILE)[:, None]
    c = np.arange(ATTN_TILE)[None, :]
    out = np.zeros((SEQ // ATTN_TILE, ATTN_TILE, ATTN_TILE), np.float32)
    for delta in range(SEQ // ATTN_TILE):
        d = delta * ATTN_TILE + r - c
        for window, dil in DILATED_PAIRS:
            out[delta] += (d >= 0) & (d <= window) & (d % dil == 0)
    return jnp.asarray(out, BF16)


def _block_indicator():
    ind = np.zeros((HEAD_PAD, SEQ), np.float32)
    for h in range(N_HEADS):
        for n in range(N_MOBA_BLOCKS):
            ind[N_MOBA_BLOCKS * h + n, n * MOBA_BLOCK:(n + 1) * MOBA_BLOCK] = 1.0
    return jnp.asarray(ind, BF16)


def _block_head_mask():
    m = np.zeros((HEAD_PAD, BRANCH_WIDTH), np.float32)
    for h in range(N_HEADS):
        m[N_MOBA_BLOCKS * h:N_MOBA_BLOCKS * (h + 1), h * HEAD_DIM:(h + 1) * HEAD_DIM] = 1.0
    return jnp.asarray(m)


def _full(shape):
    return pl.BlockSpec(shape, lambda *_: (0,) * len(shape))


def _params(*sem):
    return pltpu.CompilerParams(dimension_semantics=sem, vmem_limit_bytes=VMEM_LIMIT)


def _inproj_kernel(x_ref, g_ref, wqkv_ref, wsmall_ref, wf_ref, ra_ref, rbm_ref, rbp_ref,
                   qkv_ref, small_ref, f_ref):
    hf = _rms(x_ref[...], g_ref[...])
    h = hf.astype(BF16)
    a, bm, bp = ra_ref[...], rbm_ref[...], rbp_ref[...]
    for c in range(9):
        cols = slice(c * BRANCH_WIDTH, (c + 1) * BRANCH_WIDTH)
        y = _dot(h, wqkv_ref[:, cols])
        if c in (3, 4, 6, 7):
            y = _rope(y, a, bm, bp, ROT_DIM // 2)
        qkv_ref[:, cols] = y.astype(BF16)
    small_ref[...] = _dot(h, wsmall_ref[...])
    lane = lax.broadcasted_iota(jnp.int32, f_ref.shape, 1)
    f = jnp.zeros(f_ref.shape, F32)
    for j in range(N_HEADS):
        col = jnp.sum(hf * wf_ref[j:j + 1, :], axis=-1, keepdims=True)
        f = jnp.where(lane == j, col, f)
    f_ref[...] = f


def _inproj(x2, g, wqkv, wsmall, wf, rope_p):
    m = x2.shape[0]
    tm = TOKEN_TILE
    seq_tiles = SEQ // tm
    row = lambda i: (i, 0)
    tab = pl.BlockSpec((tm, BRANCH_WIDTH), lambda i: (i % seq_tiles, 0))
    return pl.pallas_call(
        _inproj_kernel,
        grid=(m // tm,),
        in_specs=[pl.BlockSpec((tm, D_MODEL), row), _full((1, D_MODEL)),
                  _full(wqkv.shape), _full(wsmall.shape), _full(wf.shape), tab, tab, tab],
        out_specs=[pl.BlockSpec((tm, 9 * BRANCH_WIDTH), row), pl.BlockSpec((tm, 512), row),
                   pl.BlockSpec((tm, HEAD_PAD), row)],
        out_shape=[jax.ShapeDtypeStruct((m, 9 * BRANCH_WIDTH), BF16),
                   jax.ShapeDtypeStruct((m, 512), F32),
                   jax.ShapeDtypeStruct((m, HEAD_PAD), F32)],
        compiler_params=_params("parallel"),
        name="inproj",
    )(x2, g, wqkv, wsmall, wf, *rope_p)


def _prep_kernel(*refs, mode):
    if mode == "fox":
        q_ref, k_ref, f_ref, bf_ref, pq_ref, pk_ref, pc_ref, qone_ref, kone_ref, qo_ref, ko_ref = refs
    elif mode == "moba":
        q_ref, k_ref, ind_ref, hmask_ref, pq_ref, pk_ref, psel_ref, qo_ref, ko_ref = refs
    else:
        q_ref, k_ref, pq_ref, pk_ref, qo_ref, ko_ref = refs
    chunk = 256
    n_chunks = SEQ // chunk

    if mode == "fox":
        ri = lax.broadcasted_iota(jnp.int32, (chunk, chunk), 0)
        ci = lax.broadcasted_iota(jnp.int32, (chunk, chunk), 1)
        tril = jnp.where(ri >= ci, 1.0, 0.0).astype(BF16)
        carry = jnp.zeros((1, HEAD_PAD), F32)
    if mode == "moba":
        kmean = _dot(ind_ref[...], k_ref[0]) * (1.0 / MOBA_BLOCK)
        kmean = kmean * hmask_ref[...]
        km_hi, km_mid, km_lo = _split3(kmean)
        lane = lax.broadcasted_iota(jnp.int32, (chunk, HEAD_PAD), 1)
        blk = lane % N_MOBA_BLOCKS
        lane_q = lax.broadcasted_iota(jnp.int32, (chunk, QK_PAD), 1) % HEAD_PAD

    for c in range(n_chunks):
        rows = slice(c * chunk, (c + 1) * chunk)
        q = q_ref[0, rows, :]
        k = k_ref[0, rows, :]
        qa = _dot(q, pq_ref[...])
        ka = _dot(k, pk_ref[...])
        if mode == "fox":
            z = f_ref[0, rows, :] + bf_ref[...]
            logf = jnp.minimum(z, 0.0) - jnp.log(1.0 + jnp.exp(-jnp.abs(z)))
            hi, mid, lo = _split3(logf)
            cum = _dot(tril, hi) + _dot(tril, mid) + _dot(tril, lo) + carry
            carry = cum[chunk - 1:chunk, :]
            c_hi, c_mid, c_lo = _split3(cum)
            q_bias = _dot(c_hi, pc_ref[0]) + _dot(c_mid, pc_ref[1]) + _dot(c_lo, pc_ref[2])
            k_bias = _dot(c_hi, pc_ref[3]) + _dot(c_mid, pc_ref[4]) + _dot(c_lo, pc_ref[5])
            qa = qa + q_bias + qone_ref[...]
            ka = ka - k_bias + kone_ref[...]
        elif mode == "moba":
            gate = _dot_nt(q, km_hi) + _dot_nt(q, km_mid) + _dot_nt(q, km_lo)
            own = (c * chunk) // MOBA_BLOCK
            rank = jnp.zeros((chunk, HEAD_PAD), F32)
            for r in range(1, N_MOBA_BLOCKS):
                wrap = blk + r >= N_MOBA_BLOCKS
                other = jnp.where(wrap, pltpu.roll(gate, N_MOBA_BLOCKS - r, 1),
                                  pltpu.roll(gate, HEAD_PAD - r, 1))
                other_blk = jnp.where(wrap, blk + r - N_MOBA_BLOCKS, blk + r)
                ahead = (other > gate) | ((other == gate) & (other_blk < blk))
                rank = rank + jnp.where(ahead & (other_blk < own), 1.0, 0.0)
            keep = ((blk < own) & (rank < MOBA_TOPK)) | (blk == own)
            bias = jnp.where(keep, 0.0, NEG).astype(BF16)
            qa = qa + _dot(bias, psel_ref[...])
            ka = ka + jnp.where(lane_q == AUG + own, 1.0, 0.0)
        qo_ref[0, rows, :] = qa.astype(BF16)
        ko_ref[0, rows, :] = ka.astype(BF16)


def _prep(qkv3, col0, mode, extra, consts):
    b = qkv3.shape[0]
    seq_blk = lambda cb: pl.BlockSpec((1, SEQ, BRANCH_WIDTH), lambda i, cb=cb: (i, 0, cb))
    in_specs = [seq_blk(col0), seq_blk(col0 + 1)]
    args = [qkv3, qkv3]
    for e in extra:
        in_specs.append(pl.BlockSpec((1, SEQ, e.shape[-1]), lambda i: (i, 0, 0)))
        args.append(e)
    for cst in consts:
        in_specs.append(_full(cst.shape))
        args.append(cst)
    out = jax.ShapeDtypeStruct((b, SEQ, QK_PAD), BF16)
    ospec = pl.BlockSpec((1, SEQ, QK_PAD), lambda i: (i, 0, 0))
    return pl.pallas_call(
        functools.partial(_prep_kernel, mode=mode),
        grid=(b,), in_specs=in_specs, out_specs=[ospec, ospec], out_shape=[out, out],
        compiler_params=_params("parallel"),
        name="prep_" + mode,
    )(*args)


def _mla_kernel(small_ref, gq_ref, gkv_ref, wuq_ref, wuk_ref, wuv_ref, ra_ref, rbm_ref, rbp_ref,
                q_ref, k_ref, v_ref):
    a, bm, bp = ra_ref[...], rbm_ref[...], rbp_ref[...]
    half = MLA_ROPE_DIM // 2
    scale = (MLA_NOPE_DIM + MLA_ROPE_DIM) ** -0.5
    cq = _rms(small_ref[:, 0:MLA_Q_RANK], gq_ref[...]).astype(BF16)
    ckv = _rms(small_ref[:, MLA_Q_RANK:MLA_Q_RANK + MLA_KV_RANK], gkv_ref[...]).astype(BF16)
    kr = _rope(small_ref[:, MLA_Q_RANK + MLA_KV_RANK:], a, bm, bp, half)
    qf = _dot(cq, wuq_ref[...])
    kf = _dot(ckv, wuk_ref[...])
    for h in range(N_HEADS):
        cols = slice(h * HEAD_PAD, (h + 1) * HEAD_PAD)
        q_ref[:, cols] = (_rope(qf[:, cols], a, bm, bp, half) * scale).astype(BF16)
        k_ref[:, cols] = (kf[:, cols] + kr).astype(BF16)
    v_ref[...] = _dot(ckv, wuv_ref[...]).astype(BF16)


def _mla_prep(small, gq, gkv, wuq, wuk, wuv, rope_m):
    m = small.shape[0]
    tm = TOKEN_TILE
    seq_tiles = SEQ // tm
    row = lambda i: (i, 0)
    tab = pl.BlockSpec((tm, HEAD_PAD), lambda i: (i % seq_tiles, 0))
    return pl.pallas_call(
        _mla_kernel,
        grid=(m // tm,),
        in_specs=[pl.BlockSpec((tm, 512), row), _full(gq.shape), _full(gkv.shape),
                  _full(wuq.shape), _full(wuk.shape), _full(wuv.shape), tab, tab, tab],
        out_specs=[pl.BlockSpec((tm, QK_PAD), row), pl.BlockSpec((tm, QK_PAD), row),
                   pl.BlockSpec((tm, BRANCH_WIDTH), row)],
        out_shape=[jax.ShapeDtypeStruct((m, QK_PAD), BF16), jax.ShapeDtypeStruct((m, QK_PAD), BF16),
                   jax.ShapeDtypeStruct((m, BRANCH_WIDTH), BF16)],
        compiler_params=_params("parallel"),
        name="mla_prep",
    )(small, gq, gkv, wuq, wuk, wuv, *rope_m)


def _attn_kernel(*refs, dilated):
    if dilated:
        q_ref, k_ref, v_ref, w_ref, o_ref, vm_ref = refs
    else:
        q_ref, k_ref, v_ref, o_ref, vm_ref = refs
    t = ATTN_TILE
    i = pl.program_id(1)
    lane = lax.broadcasted_iota(jnp.int32, (t, BRANCH_WIDTH), 1)

    @pl.when(i == 0)
    def _():
        v = v_ref[0]
        vlane = lax.broadcasted_iota(jnp.int32, v.shape, 1)
        for h in range(N_HEADS):
            vm_ref[h] = jnp.where(vlane // HEAD_DIM == h, v, jnp.zeros_like(v))

    q = q_ref[0]
    causal = (lax.broadcasted_iota(jnp.int32, (t, t), 0) >= lax.broadcasted_iota(jnp.int32, (t, t), 1))

    def by_head(vals):
        out = vals[N_HEADS - 1]
        for h in range(N_HEADS - 2, -1, -1):
            out = jnp.where(lane < (h + 1) * HEAD_DIM, vals[h], out)
        return out

    def step(j, carry, diag):
        ms, ls, acc = carry
        off = pl.multiple_of(j * t, t)
        k_t = k_ref[0, pl.ds(off, t), :]
        if dilated:
            w = w_ref[i - j].astype(F32)
        new_ms, new_ls, alphas = [], [], []
        pv = None
        for h in range(N_HEADS):
            cols = slice(h * HEAD_PAD, (h + 1) * HEAD_PAD)
            s = _dot_nt(q[:, cols], k_t[:, cols])
            if dilated:
                s = jnp.where(w > 0.0, s, NEG)
            elif diag:
                s = jnp.where(causal, s, NEG)
            m_new = jnp.maximum(ms[h], jnp.max(s, axis=-1, keepdims=True))
            alpha = jnp.exp(ms[h] - m_new)
            p = jnp.exp(s - m_new)
            if dilated:
                p = p * w
            new_ms.append(m_new)
            new_ls.append(alpha * ls[h] + jnp.sum(p, axis=-1, keepdims=True))
            alphas.append(alpha)
            d = _dot(p.astype(BF16), vm_ref[h, pl.ds(off, t), :])
            pv = d if pv is None else pv + d
        return tuple(new_ms), tuple(new_ls), acc * by_head(alphas) + pv

    init = (tuple(jnp.full((t, 1), NEG, F32) for _ in range(N_HEADS)),
            tuple(jnp.zeros((t, 1), F32) for _ in range(N_HEADS)),
            jnp.zeros((t, BRANCH_WIDTH), F32))
    if dilated:
        ms, ls, acc = lax.fori_loop(0, i + 1, functools.partial(step, diag=False), init)
    else:
        carry = lax.fori_loop(0, i, functools.partial(step, diag=False), init)
        ms, ls, acc = step(i, carry, True)
    o_ref[0] = (acc * by_head([1.0 / l for l in ls])).astype(o_ref.dtype)


def _attention(q_aug, k_aug, v_src, v_col, dil_w=None):
    b = q_aug.shape[0]
    t = ATTN_TILE
    in_specs = [pl.BlockSpec((1, t, QK_PAD), lambda bi, i: (bi, i, 0)),
                pl.BlockSpec((1, SEQ, QK_PAD), lambda bi, i: (bi, 0, 0)),
                pl.BlockSpec((1, SEQ, BRANCH_WIDTH), lambda bi, i: (bi, 0, v_col))]
    args = [q_aug, k_aug, v_src]
    if dil_w is not None:
        in_specs.append(_full(dil_w.shape))
        args.append(dil_w)
    return pl.pallas_call(
        functools.partial(_attn_kernel, dilated=dil_w is not None),
        grid=(b, SEQ // t),
        in_specs=in_specs,
        out_specs=pl.BlockSpec((1, t, BRANCH_WIDTH), lambda bi, i: (bi, i, 0)),
        out_shape=jax.ShapeDtypeStruct((b, SEQ, BRANCH_WIDTH), BF16),
        scratch_shapes=[pltpu.VMEM((N_HEADS, SEQ, BRANCH_WIDTH), BF16)],
        compiler_params=_params("parallel", "arbitrary"),
        name="attn_dil" if dil_w is not None else "attn",
    )(*args)


def _merge_kernel(x_ref, gpre_ref, wg_ref, y0_ref, y1_ref, y2_ref, y3_ref, wb_ref, wo_ref, gpost_ref,
                  o_ref):
    x = x_ref[...]
    h = _rms(x, gpre_ref[...]).astype(BF16)
    merged = None
    for n, y_ref in enumerate((y0_ref, y1_ref, y2_ref, y3_ref)):
        gate = jax.nn.sigmoid(_dot(h, wg_ref[:, n * D_MODEL:(n + 1) * D_MODEL]))
        term = gate * _dot(y_ref[...], wb_ref[n])
        merged = term if merged is None else merged + term
    mix = _dot(merged.astype(BF16), wo_ref[...])
    o_ref[...] = x + _rms(mix, gpost_ref[...])


def _merge(x2, gpre, wg, ys, wb, wo, gpost):
    m = x2.shape[0]
    tm = TOKEN_TILE
    row = lambda i: (i, 0)
    yspec = pl.BlockSpec((tm, BRANCH_WIDTH), row)
    return pl.pallas_call(
        _merge_kernel,
        grid=(m // tm,),
        in_specs=[pl.BlockSpec((tm, D_MODEL), row), _full(gpre.shape), _full(wg.shape),
                  yspec, yspec, yspec, yspec, _full(wb.shape), _full(wo.shape), _full(gpost.shape)],
        out_specs=pl.BlockSpec((tm, D_MODEL), row),
        out_shape=jax.ShapeDtypeStruct((m, D_MODEL), F32),
        compiler_params=_params("parallel"),
        name="merge",
    )(x2, gpre, wg, *ys, wb, wo, gpost)


def _mlp_kernel(x_ref, gpre_ref, wup_ref, wdown_ref, gpost_ref, o_ref):
    x = x_ref[...]
    h = _rms(x, gpre_ref[...]).astype(BF16)
    acc = None
    for c in range(D_FF // FF_CHUNK):
        cols = slice(c * FF_CHUNK, (c + 1) * FF_CHUNK)
        up = jnp.maximum(_dot(h, wup_ref[:, cols]), 0.0)
        term = _dot((up * up).astype(BF16), wdown_ref[cols, :])
        acc = term if acc is None else acc + term
    o_ref[...] = x + _rms(acc, gpost_ref[...])


def _mlp(x2, gpre, wup, wdown, gpost):
    m = x2.shape[0]
    tm = TOKEN_TILE
    row = lambda i: (i, 0)
    return pl.pallas_call(
        _mlp_kernel,
        grid=(m // tm,),
        in_specs=[pl.BlockSpec((tm, D_MODEL), row), _full(gpre.shape), _full(wup.shape),
                  _full(wdown.shape), _full(gpost.shape)],
        out_specs=pl.BlockSpec((tm, D_MODEL), row),
        out_shape=jax.ShapeDtypeStruct((m, D_MODEL), F32),
        compiler_params=_params("parallel"),
        name="mlp",
    )(x2, gpre, wup, wdown, gpost)


def _pad_heads(w, width, src_lo, src_hi, dst_lo):
    out = jnp.zeros((w.shape[0], QK_PAD), w.dtype)
    for h in range(N_HEADS):
        out = out.at[:, h * HEAD_PAD + dst_lo:h * HEAD_PAD + dst_lo + src_hi - src_lo].set(
            w[:, h * width + src_lo:h * width + src_hi])
    return out


def kernel(x, w_in, b_forget, g_cq, g_ckv, w_uq, w_uk, w_uv, w_branch, w_out, w_up, w_down,
           g_pre_mix, g_post_mix, g_pre_mlp, g_post_mlp):
    b, s, d = x.shape
    assert (s, d) == (SEQ, D_MODEL)
    depth = w_in.shape[0]
    m = b * s
    rope_p, rope_m = _rope_tables()
    place_q = _head_placement(HEAD_DIM ** -0.5)
    place_k = _head_placement(1.0)
    fox_consts = [place_q, place_k, jnp.stack([_lane_placement(1, j) for j in range(6)]),
                  _aug_ones(3, 6), _aug_ones(0, 3)]
    moba_consts = [_block_indicator(), _block_head_mask(), place_q, place_k,
                   _lane_placement(N_MOBA_BLOCKS, 0)]
    dil_w = _dilated_weights()

    q0 = 3 * BRANCH_WIDTH
    o_f = q0
    o_dil = o_f + N_HEADS
    o_moba = o_dil + q0
    o_cq = o_moba + q0
    o_ckv = o_cq + MLA_Q_RANK
    o_kr = o_ckv + MLA_KV_RANK
    o_gate = o_kr + MLA_ROPE_DIM

    x2 = x.reshape(m, d)
    vec = lambda g: g.reshape(1, -1)
    for l in range(depth):
        wl = w_in[l]
        wqkv = jnp.concatenate([wl[:, 0:q0], wl[:, o_dil:o_dil + q0], wl[:, o_moba:o_moba + q0]],
                               axis=1).astype(BF16)
        kr_pad = jnp.zeros((d, HEAD_PAD), F32).at[:, MLA_NOPE_DIM:MLA_NOPE_DIM + MLA_ROPE_DIM].set(
            wl[:, o_kr:o_gate])
        wsmall = jnp.concatenate([wl[:, o_cq:o_kr], kr_pad], axis=1).astype(BF16)
        wf = jnp.zeros((8, d), F32).at[:N_HEADS].set(wl[:, o_f:o_dil].T)
        wgate = wl[:, o_gate:].astype(BF16)
        bf = jnp.zeros((1, HEAD_PAD), F32).at[0, :N_HEADS].set(b_forget[l])
        per_head = MLA_NOPE_DIM + MLA_ROPE_DIM
        wuq = _pad_heads(w_uq[l], per_head, 0, per_head, 0).astype(BF16)
        wuk = _pad_heads(w_uk[l], MLA_NOPE_DIM, 0, MLA_NOPE_DIM, 0).astype(BF16)

        qkv, small, fpre = _inproj(x2, vec(g_pre_mix[l]), wqkv, wsmall, wf, rope_p)
        qkv3 = qkv.reshape(b, s, 9 * BRANCH_WIDTH)
        fq, fk = _prep(qkv3, 0, "fox", [fpre.reshape(b, s, HEAD_PAD)], [bf] + fox_consts)
        dq, dk = _prep(qkv3, 3, "dil", [], [place_q, place_k])
        mq, mk = _prep(qkv3, 6, "moba", [], moba_consts)
        lq, lk, lv = _mla_prep(small, vec(g_cq[l]), vec(g_ckv[l]), wuq, wuk, w_uv[l].astype(BF16), rope_m)
        y_fox = _attention(fq, fk, qkv3, 2)
        y_dil = _attention(dq, dk, qkv3, 5, dil_w)
        y_moba = _attention(mq, mk, qkv3, 8)
        y_mla = _attention(lq.reshape(b, s, QK_PAD), lk.reshape(b, s, QK_PAD),
                           lv.reshape(b, s, BRANCH_WIDTH), 0)
        ys = [y.reshape(m, BRANCH_WIDTH) for y in (y_fox, y_dil, y_moba, y_mla)]
        x2 = _merge(x2, vec(g_pre_mix[l]), wgate, ys, w_branch[l].astype(BF16), w_out[l].astype(BF16),
                    vec(g_post_mix[l]))
        x2 = _mlp(x2, vec(g_pre_mlp[l]), w_up[l].astype(BF16), w_down[l].astype(BF16),
                  vec(g_post_mlp[l]))
    return x2.reshape(b, s, d)
```

```python
import functools

import numpy as np
import jax
import jax.numpy as jnp
from jax import lax
from jax.experimental import pallas as pl
from jax.experimental.pallas import tpu as pltpu

D_MODEL = 1024
SEQ = 2048
HEAD_DIM = 64
N_HEADS = 4
BRANCH_WIDTH = N_HEADS * HEAD_DIM
HEAD_PAD = 128
QK_PAD = N_HEADS * HEAD_PAD
AUG = HEAD_DIM
ROT_DIM = HEAD_DIM // 4
ROPE_THETA = 500000.0
DILATED_PAIRS = ((128, 1), (512, 4), (2048, 16))
MOBA_BLOCK = 256
MOBA_TOPK = 3
N_MOBA_BLOCKS = SEQ // MOBA_BLOCK
MLA_Q_RANK = 256
MLA_KV_RANK = 128
MLA_NOPE_DIM = 64
MLA_ROPE_DIM = 32
D_FF = 4 * D_MODEL
NORM_EPS = 1e-6
NEG = -1e30

TOKEN_TILE = 512
ATTN_TILE = 256
ACC_ROWS = HEAD_DIM + 16
LOG2E = 1.4426950408889634
FF_CHUNK = 512
VMEM_LIMIT = 56 * 1024 * 1024

F32 = jnp.float32
BF16 = jnp.bfloat16


def _dot(a, b):
    return jnp.dot(a, b, preferred_element_type=F32)


def _dot_nt(a, b):
    return lax.dot_general(a, b, (((1,), (1,)), ((), ())), preferred_element_type=F32)


def _rms(x, g):
    return x * lax.rsqrt(jnp.mean(x * x, axis=-1, keepdims=True) + NORM_EPS) * g


def _rope(y, a, bm, bp, half, axis):
    n = y.shape[axis]
    return y * a + pltpu.roll(y, n - half, axis) * bm + pltpu.roll(y, half, axis) * bp


def _split3(x):
    hi = x.astype(BF16)
    r = x - hi.astype(F32)
    mid = r.astype(BF16)
    lo = (r - mid.astype(F32)).astype(BF16)
    return hi, mid, lo


def _rope_tables():
    def tables(dim):
        inv_freq = 1.0 / (ROPE_THETA ** (jnp.arange(0, dim, 2, dtype=F32) / dim))
        ang = jnp.arange(SEQ, dtype=F32)[:, None] * inv_freq[None, :]
        return jnp.cos(ang), jnp.sin(ang)

    def expand(cos, sin, width, period, start, half):
        pos = np.arange(width) % period - start
        first = (pos >= 0) & (pos < half)
        second = (pos >= half) & (pos < 2 * half)
        idx = np.where(first, pos, np.where(second, pos - half, 0))
        a = jnp.where((first | second)[None, :], cos[:, idx], 1.0)
        bm = jnp.where(first[None, :], -sin[:, idx], 0.0)
        bp = jnp.where(second[None, :], sin[:, idx], 0.0)
        return a, bm, bp

    cos_p, sin_p = tables(ROT_DIM)
    cos_m, sin_m = tables(MLA_ROPE_DIM)
    partial = expand(cos_p, sin_p, BRANCH_WIDTH, HEAD_DIM, 0, ROT_DIM // 2)
    mla = expand(cos_m, sin_m, HEAD_PAD, HEAD_PAD, MLA_NOPE_DIM, MLA_ROPE_DIM // 2)
    return partial, mla


def _head_placement():
    p = np.zeros((BRANCH_WIDTH, QK_PAD), np.float32)
    for h in range(N_HEADS):
        for d in range(HEAD_DIM):
            p[h * HEAD_DIM + d, h * HEAD_PAD + d] = 1.0
    return jnp.asarray(p, BF16)


def _lane_placement(per_head, offset):
    p = np.zeros((HEAD_PAD, QK_PAD), np.float32)
    for h in range(N_HEADS):
        for n in range(per_head):
            p[per_head * h + n, h * HEAD_PAD + AUG + offset + n] = 1.0
    return jnp.asarray(p, BF16)


def _aug_ones(lo, hi):
    c = np.zeros((1, QK_PAD), np.float32)
    for h in range(N_HEADS):
        c[0, h * HEAD_PAD + AUG + lo:h * HEAD_PAD + AUG + hi] = 1.0
    return jnp.asarray(c)


def _dilated_bias():
    r = np.arange(ATTN_TILE)[None, :]
    c = np.arange(ATTN_TILE)[:, None]
    count = np.zeros((SEQ // ATTN_TILE, ATTN_TILE, ATTN_TILE), np.float32)
    for delta in range(SEQ // ATTN_TILE):
        d = delta * ATTN_TILE + r - c
        for window, dil in DILATED_PAIRS:
            count[delta] += (d >= 0) & (d <= window) & (d % dil == 0)
    return jnp.asarray(np.where(count > 0, np.log2(np.maximum(count, 1.0)), NEG).astype(np.float32))


def _block_indicator():
    ind = np.zeros((HEAD_PAD, SEQ), np.float32)
    for h in range(N_HEADS):
        for n in range(N_MOBA_BLOCKS):
            ind[N_MOBA_BLOCKS * h + n, n * MOBA_BLOCK:(n + 1) * MOBA_BLOCK] = 1.0
    return jnp.asarray(ind, BF16)


def _block_head_mask():
    m = np.zeros((HEAD_PAD, BRANCH_WIDTH), np.float32)
    for h in range(N_HEADS):
        m[N_MOBA_BLOCKS * h:N_MOBA_BLOCKS * (h + 1), h * HEAD_DIM:(h + 1) * HEAD_DIM] = 1.0
    return jnp.asarray(m)


def _full(shape):
    return pl.BlockSpec(shape, lambda *_: (0,) * len(shape))


def _params(*sem):
    return pltpu.CompilerParams(dimension_semantics=sem, vmem_limit_bytes=VMEM_LIMIT)


def _inproj_kernel(x_ref, g_ref, wqk_ref, wvt_ref, wsmall_ref, wf_ref, ra_ref, rbm_ref, rbp_ref,
                   qk_ref, vt_ref, small_ref, f_ref):
    hf = _rms(x_ref[...], g_ref[...])
    h = hf.astype(BF16)
    a, bm, bp = ra_ref[...], rbm_ref[...], rbp_ref[...]
    for c in range(6):
        cols = slice(c * BRANCH_WIDTH, (c + 1) * BRANCH_WIDTH)
        y = _dot(h, wqk_ref[:, cols])
        if c >= 2:
            y = _rope(y, a, bm, bp, ROT_DIM // 2, 1)
        if c % 2 == 0:
            y = y * (HEAD_DIM ** -0.5 * LOG2E)
        qk_ref[:, cols] = y.astype(BF16)
    for c in range(3):
        rows = slice(c * BRANCH_WIDTH, (c + 1) * BRANCH_WIDTH)
        vt_ref[0, rows, :] = _dot_nt(wvt_ref[rows, :], h).astype(BF16)
    small_ref[...] = _dot(h, wsmall_ref[...])
    lane = lax.broadcasted_iota(jnp.int32, f_ref.shape, 1)
    f = jnp.zeros(f_ref.shape, F32)
    for j in range(N_HEADS):
        col = jnp.sum(hf * wf_ref[j:j + 1, :], axis=-1, keepdims=True)
        f = jnp.where(lane == j, col, f)
    f_ref[...] = f


def _inproj(x2, g, wqk, wvt, wsmall, wf, rope_p):
    m = x2.shape[0]
    tm = TOKEN_TILE
    seq_tiles = SEQ // tm
    row = lambda i: (i, 0)
    tab = pl.BlockSpec((tm, BRANCH_WIDTH), lambda i: (i % seq_tiles, 0))
    return pl.pallas_call(
        _inproj_kernel,
        grid=(m // tm,),
        in_specs=[pl.BlockSpec((tm, D_MODEL), row), _full((1, D_MODEL)),
                  _full(wqk.shape), _full(wvt.shape), _full(wsmall.shape), _full(wf.shape), tab, tab, tab],
        out_specs=[pl.BlockSpec((tm, 6 * BRANCH_WIDTH), row),
                   pl.BlockSpec((1, 3 * BRANCH_WIDTH, tm), lambda i: (i // seq_tiles, 0, i % seq_tiles)),
                   pl.BlockSpec((tm, 512), row), pl.BlockSpec((tm, HEAD_PAD), row)],
        out_shape=[jax.ShapeDtypeStruct((m, 6 * BRANCH_WIDTH), BF16),
                   jax.ShapeDtypeStruct((m // SEQ, 3 * BRANCH_WIDTH, SEQ), BF16),
                   jax.ShapeDtypeStruct((m, 512), F32),
                   jax.ShapeDtypeStruct((m, HEAD_PAD), F32)],
        compiler_params=_params("parallel"),
        name="inproj",
    )(x2, g, wqk, wvt, wsmall, wf, *rope_p)


def _prep_kernel(*refs, mode):
    if mode == "fox":
        (q_ref, k_ref, f_ref, bf_ref, pq_ref, pk_ref, pcq_ref, pck_ref, qone_ref, kone_ref,
         qo_ref, ko_ref) = refs
    elif mode == "moba":
        q_ref, k_ref, ind_ref, hmask_ref, pq_ref, pk_ref, psel_ref, qo_ref, ko_ref = refs
    else:
        q_ref, k_ref, pq_ref, pk_ref, qo_ref, ko_ref = refs
    chunk = 256
    n_chunks = SEQ // chunk

    if mode == "fox":
        ri = lax.broadcasted_iota(jnp.int32, (chunk, chunk), 0)
        ci = lax.broadcasted_iota(jnp.int32, (chunk, chunk), 1)
        tril = jnp.where(ri >= ci, 1.0, 0.0).astype(BF16)
        carry = jnp.zeros((1, HEAD_PAD), F32)
    if mode == "moba":
        kmean = _dot(ind_ref[...], k_ref[0]) * (1.0 / MOBA_BLOCK)
        kmean = kmean * hmask_ref[...]
        km_hi, km_mid, km_lo = _split3(kmean)
        lane = lax.broadcasted_iota(jnp.int32, (chunk, HEAD_PAD), 1)
        blk = lane % N_MOBA_BLOCKS
        lane_q = lax.broadcasted_iota(jnp.int32, (chunk, QK_PAD), 1) % HEAD_PAD

    for c in range(n_chunks):
        rows = slice(c * chunk, (c + 1) * chunk)
        q = q_ref[0, rows, :]
        k = k_ref[0, rows, :]
        qa = _dot_nt(pq_ref[...], q)
        ka = _dot(k, pk_ref[...])
        if mode == "fox":
            z = f_ref[0, rows, :] + bf_ref[...]
            logf = jnp.minimum(z, 0.0) - jnp.log(1.0 + jnp.exp(-jnp.abs(z)))
            hi, mid, lo = _split3(logf)
            cum = _dot(tril, hi) + _dot(tril, mid) + _dot(tril, lo) + carry
            carry = cum[chunk - 1:chunk, :]
            c_hi, c_mid, c_lo = _split3(cum * LOG2E)
            q_bias = _dot_nt(pcq_ref[0], c_hi) + _dot_nt(pcq_ref[1], c_mid) + _dot_nt(pcq_ref[2], c_lo)
            k_bias = _dot(c_hi, pck_ref[0]) + _dot(c_mid, pck_ref[1]) + _dot(c_lo, pck_ref[2])
            qa = qa + q_bias + qone_ref[...]
            ka = ka - k_bias + kone_ref[...]
        elif mode == "moba":
            gate = _dot_nt(q, km_hi) + _dot_nt(q, km_mid) + _dot_nt(q, km_lo)
            own = (c * chunk) // MOBA_BLOCK
            rank = jnp.zeros((chunk, HEAD_PAD), F32)
            for r in range(1, N_MOBA_BLOCKS):
                wrap = blk + r >= N_MOBA_BLOCKS
                other = jnp.where(wrap, pltpu.roll(gate, N_MOBA_BLOCKS - r, 1),
                                  pltpu.roll(gate, HEAD_PAD - r, 1))
                other_blk = jnp.where(wrap, blk + r - N_MOBA_BLOCKS, blk + r)
                ahead = (other > gate) | ((other == gate) & (other_blk < blk))
                rank = rank + jnp.where(ahead & (other_blk < own), 1.0, 0.0)
            keep = ((blk < own) & (rank < MOBA_TOPK)) | (blk == own)
            bias = jnp.where(keep, 0.0, NEG).astype(BF16)
            qa = qa + _dot_nt(psel_ref[...], bias)
            ka = ka + jnp.where(lane_q == AUG + own, 1.0, 0.0)
        qo_ref[0, :, rows] = qa.astype(BF16)
        ko_ref[0, rows, :] = ka.astype(BF16)


def _prep(qkv3, col0, mode, extra, consts):
    b = qkv3.shape[0]
    seq_blk = lambda cb: pl.BlockSpec((1, SEQ, BRANCH_WIDTH), lambda i, cb=cb: (i, 0, cb))
    in_specs = [seq_blk(col0), seq_blk(col0 + 1)]
    args = [qkv3, qkv3]
    for e in extra:
        in_specs.append(pl.BlockSpec((1, SEQ, e.shape[-1]), lambda i: (i, 0, 0)))
        args.append(e)
    for cst in consts:
        in_specs.append(_full(cst.shape))
        args.append(cst)
    out_shape = [jax.ShapeDtypeStruct((b, QK_PAD, SEQ), BF16), jax.ShapeDtypeStruct((b, SEQ, QK_PAD), BF16)]
    out_specs = [pl.BlockSpec((1, QK_PAD, SEQ), lambda i: (i, 0, 0)),
                 pl.BlockSpec((1, SEQ, QK_PAD), lambda i: (i, 0, 0))]
    return pl.pallas_call(
        functools.partial(_prep_kernel, mode=mode),
        grid=(b,), in_specs=in_specs, out_specs=out_specs, out_shape=out_shape,
        compiler_params=_params("parallel"),
        name="prep_" + mode,
    )(*args)


def _mla_kernel(small_ref, gq_ref, gkv_ref, wuqt_ref, wuk_ref, wuvt_ref, ra_ref, rbm_ref, rbp_ref,
                rat_ref, rbmt_ref, rbpt_ref, qt_ref, k_ref, vt_ref):
    half = MLA_ROPE_DIM // 2
    scale = (MLA_NOPE_DIM + MLA_ROPE_DIM) ** -0.5 * LOG2E
    cq = _rms(small_ref[:, 0:MLA_Q_RANK], gq_ref[...]).astype(BF16)
    ckv = _rms(small_ref[:, MLA_Q_RANK:MLA_Q_RANK + MLA_KV_RANK], gkv_ref[...]).astype(BF16)
    kr = _rope(small_ref[:, MLA_Q_RANK + MLA_KV_RANK:], ra_ref[...], rbm_ref[...], rbp_ref[...],
               half, 1)
    qf_t = _dot_nt(wuqt_ref[...], cq)
    kf = _dot(ckv, wuk_ref[...])
    at, bmt, bpt = rat_ref[...], rbmt_ref[...], rbpt_ref[...]
    for h in range(N_HEADS):
        cols = slice(h * HEAD_PAD, (h + 1) * HEAD_PAD)
        qt_ref[0, cols, :] = (_rope(qf_t[cols, :], at, bmt, bpt, half, 0) * scale).astype(BF16)
        k_ref[:, cols] = (kf[:, cols] + kr).astype(BF16)
    vt_ref[0] = _dot_nt(wuvt_ref[...], ckv).astype(BF16)


def _mla_prep(small, gq, gkv, wuqt, wuk, wuvt, rope_m):
    m = small.shape[0]
    tm = TOKEN_TILE
    seq_tiles = SEQ // tm
    row = lambda i: (i, 0)
    by_batch = lambda i: (i // seq_tiles, 0, i % seq_tiles)
    tab = pl.BlockSpec((tm, HEAD_PAD), lambda i: (i % seq_tiles, 0))
    tab_t = pl.BlockSpec((HEAD_PAD, tm), lambda i: (0, i % seq_tiles))
    return pl.pallas_call(
        _mla_kernel,
        grid=(m // tm,),
        in_specs=[pl.BlockSpec((tm, 512), row), _full(gq.shape), _full(gkv.shape),
                  _full(wuqt.shape), _full(wuk.shape), _full(wuvt.shape), tab, tab, tab, tab_t, tab_t, tab_t],
        out_specs=[pl.BlockSpec((1, QK_PAD, tm), by_batch), pl.BlockSpec((tm, QK_PAD), row),
                   pl.BlockSpec((1, BRANCH_WIDTH, tm), by_batch)],
        out_shape=[jax.ShapeDtypeStruct((m // SEQ, QK_PAD, SEQ), BF16),
                   jax.ShapeDtypeStruct((m, QK_PAD), BF16),
                   jax.ShapeDtypeStruct((m // SEQ, BRANCH_WIDTH, SEQ), BF16)],
        compiler_params=_params("parallel"),
        name="mla_prep",
    )(small, gq, gkv, wuqt, wuk, wuvt, *rope_m, *[t.T for t in rope_m])


def _attn_kernel(*refs, dilated):
    qt_ref, k_ref, vt_ref, tab_ref, o_ref, sa_ref, sb_ref, m_ref, acc_ref = refs
    t = ATTN_TILE
    n_tiles = SEQ // t
    ones_rows = jnp.ones((ACC_ROWS - HEAD_DIM, t), BF16)

    m_ref[...] = jnp.full(m_ref.shape, NEG, F32)
    acc_ref[...] = jnp.zeros(acc_ref.shape, F32)

    def scores(i, j, s_ref):
        qo = pl.multiple_of(i * t, t)
        ko = pl.multiple_of(j * t, t)
        for h in range(N_HEADS):
            cols = slice(h * HEAD_PAD, (h + 1) * HEAD_PAD)
            s_ref[h] = _dot(k_ref[0, pl.ds(ko, t), cols], qt_ref[0, cols, pl.ds(qo, t)])

    def update(i, j, s_ref, bias):
        qo = pl.multiple_of(i * t, t)
        ko = pl.multiple_of(j * t, t)
        for h in range(N_HEADS):
            s = s_ref[h] if bias is None else s_ref[h] + bias
            m_old = m_ref[h, :, pl.ds(qo, t)]
            m_new = jnp.maximum(m_old, jnp.max(s, axis=0, keepdims=True))
            m_ref[h, :, pl.ds(qo, t)] = m_new
            p = jnp.exp2(s - m_new).astype(BF16)
            v_aug = jnp.concatenate([vt_ref[0, h * HEAD_DIM:(h + 1) * HEAD_DIM, pl.ds(ko, t)], ones_rows], axis=0)
            rows = slice(h * ACC_ROWS, (h + 1) * ACC_ROWS)
            acc_ref[rows, pl.ds(qo, t)] = (acc_ref[rows, pl.ds(qo, t)] * jnp.exp2(m_old - m_new)
                                           + _dot(v_aug, p))

    def pipeline(first, succ, n_steps, bias_of):
        def two_steps(_, ij):
            i0, j0 = ij
            i1, j1 = succ(i0, j0)
            i2, j2 = succ(i1, j1)
            scores(i1, j1, sb_ref)
            update(i0, j0, sa_ref, bias_of(i0, j0))
            scores(i2, j2, sa_ref)
            update(i1, j1, sb_ref, bias_of(i1, j1))
            return i2, j2

        scores(*first, sa_ref)
        lax.fori_loop(0, n_steps // 2, two_steps, (jnp.int32(first[0]), jnp.int32(first[1])))

    def succ_diag(i, j):
        nxt = jnp.minimum(i + 1, n_tiles - 1)
        return nxt, nxt

    def succ_below(i, j):
        last = j == i - 1
        return jnp.where(last, jnp.minimum(i + 1, n_tiles - 1), i), jnp.where(last, 0, j + 1)

    pipeline((0, 0), succ_diag, n_tiles, lambda i, j: tab_ref[0])
    pipeline((1, 0), succ_below, n_tiles * (n_tiles - 1) // 2,
             (lambda i, j: tab_ref[i - j]) if dilated else (lambda i, j: None))

    for i in range(n_tiles):
        q_cols = slice(i * t, (i + 1) * t)
        heads = []
        for h in range(N_HEADS):
            denom = acc_ref[h * ACC_ROWS + HEAD_DIM:h * ACC_ROWS + HEAD_DIM + 1, q_cols]
            heads.append(acc_ref[h * ACC_ROWS:h * ACC_ROWS + HEAD_DIM, q_cols] * (1.0 / denom))
        o_ref[0, q_cols, :] = jnp.concatenate(heads, axis=0).T.astype(o_ref.dtype)


def _causal_bias():
    key = np.arange(ATTN_TILE)[:, None]
    query = np.arange(ATTN_TILE)[None, :]
    return jnp.asarray(np.where(key <= query, 0.0, NEG).astype(np.float32)[None])


def _attention(qt_aug, k_aug, vt_src, v_row, table, dilated):
    b = qt_aug.shape[0]
    t = ATTN_TILE
    return pl.pallas_call(
        functools.partial(_attn_kernel, dilated=dilated),
        grid=(b,),
        in_specs=[pl.BlockSpec((1, QK_PAD, SEQ), lambda bi: (bi, 0, 0)),
                  pl.BlockSpec((1, SEQ, QK_PAD), lambda bi: (bi, 0, 0)),
                  pl.BlockSpec((1, BRANCH_WIDTH, SEQ), lambda bi: (bi, v_row, 0)),
                  _full(table.shape)],
        out_specs=pl.BlockSpec((1, SEQ, BRANCH_WIDTH), lambda bi: (bi, 0, 0)),
        out_shape=jax.ShapeDtypeStruct((b, SEQ, BRANCH_WIDTH), BF16),
        scratch_shapes=[pltpu.VMEM((N_HEADS, t, t), F32), pltpu.VMEM((N_HEADS, t, t), F32),
                        pltpu.VMEM((N_HEADS, 1, SEQ), F32), pltpu.VMEM((N_HEADS * ACC_ROWS, SEQ), F32)],
        compiler_params=_params("parallel"),
        name="attn_dil" if dilated else "attn",
    )(qt_aug, k_aug, vt_src, table)


def _merge_kernel(x_ref, gpre_ref, wg_ref, y0_ref, y1_ref, y2_ref, y3_ref, wb_ref, wo_ref, gpost_ref,
                  o_ref):
    x = x_ref[...]
    h = _rms(x, gpre_ref[...]).astype(BF16)
    merged = None
    for n, y_ref in enumerate((y0_ref, y1_ref, y2_ref, y3_ref)):
        gate = jax.nn.sigmoid(_dot(h, wg_ref[:, n * D_MODEL:(n + 1) * D_MODEL]))
        term = gate * _dot(y_ref[...], wb_ref[n])
        merged = term if merged is None else merged + term
    mix = _dot(merged.astype(BF16), wo_ref[...])
    o_ref[...] = x + _rms(mix, gpost_ref[...])


def _merge(x2, gpre, wg, ys, wb, wo, gpost):
    m = x2.shape[0]
    tm = TOKEN_TILE
    row = lambda i: (i, 0)
    yspec = pl.BlockSpec((tm, BRANCH_WIDTH), row)
    return pl.pallas_call(
        _merge_kernel,
        grid=(m // tm,),
        in_specs=[pl.BlockSpec((tm, D_MODEL), row), _full(gpre.shape), _full(wg.shape),
                  yspec, yspec, yspec, yspec, _full(wb.shape), _full(wo.shape), _full(gpost.shape)],
        out_specs=pl.BlockSpec((tm, D_MODEL), row),
        out_shape=jax.ShapeDtypeStruct((m, D_MODEL), F32),
        compiler_params=_params("parallel"),
        name="merge",
    )(x2, gpre, wg, *ys, wb, wo, gpost)


def _mlp_kernel(x_ref, gpre_ref, wup_ref, wdown_ref, gpost_ref, o_ref):
    x = x_ref[...]
    h = _rms(x, gpre_ref[...]).astype(BF16)
    acc = None
    for c in range(D_FF // FF_CHUNK):
        cols = slice(c * FF_CHUNK, (c + 1) * FF_CHUNK)
        up = jnp.maximum(_dot(h, wup_ref[:, cols]), 0.0)
        term = _dot((up * up).astype(BF16), wdown_ref[cols, :])
        acc = term if acc is None else acc + term
    o_ref[...] = x + _rms(acc, gpost_ref[...])


def _mlp(x2, gpre, wup, wdown, gpost):
    m = x2.shape[0]
    tm = TOKEN_TILE
    row = lambda i: (i, 0)
    return pl.pallas_call(
        _mlp_kernel,
        grid=(m // tm,),
        in_specs=[pl.BlockSpec((tm, D_MODEL), row), _full(gpre.shape), _full(wup.shape),
                  _full(wdown.shape), _full(gpost.shape)],
        out_specs=pl.BlockSpec((tm, D_MODEL), row),
        out_shape=jax.ShapeDtypeStruct((m, D_MODEL), F32),
        compiler_params=_params("parallel"),
        name="mlp",
    )(x2, gpre, wup, wdown, gpost)


def _pad_heads(w, width, src_lo, src_hi, dst_lo):
    out = jnp.zeros((w.shape[0], QK_PAD), w.dtype)
    for h in range(N_HEADS):
        out = out.at[:, h * HEAD_PAD + dst_lo:h * HEAD_PAD + dst_lo + src_hi - src_lo].set(
            w[:, h * width + src_lo:h * width + src_hi])
    return out


def kernel(x, w_in, b_forget, g_cq, g_ckv, w_uq, w_uk, w_uv, w_branch, w_out, w_up, w_down,
           g_pre_mix, g_post_mix, g_pre_mlp, g_post_mlp):
    b, s, d = x.shape
    assert (s, d) == (SEQ, D_MODEL)
    depth = w_in.shape[0]
    m = b * s
    rope_p, rope_m = _rope_tables()
    place_qt = _head_placement().T
    place_k = _head_placement()
    fox_consts = [place_qt, place_k, jnp.stack([_lane_placement(1, j).T for j in range(3)]),
                  jnp.stack([_lane_placement(1, j) for j in range(3, 6)]),
                  _aug_ones(3, 6).T, _aug_ones(0, 3)]
    moba_consts = [_block_indicator(), _block_head_mask(), place_qt, place_k,
                   _lane_placement(N_MOBA_BLOCKS, 0).T]
    dil_bias = _dilated_bias()
    causal_bias = _causal_bias()

    q0 = 3 * BRANCH_WIDTH
    o_f = q0
    o_dil = o_f + N_HEADS
    o_moba = o_dil + q0
    o_cq = o_moba + q0
    o_ckv = o_cq + MLA_Q_RANK
    o_kr = o_ckv + MLA_KV_RANK
    o_gate = o_kr + MLA_ROPE_DIM

    x2 = x.reshape(m, d)
    vec = lambda g: g.reshape(1, -1)
    for l in range(depth):
        wl = w_in[l]
        qk_w = 2 * BRANCH_WIDTH
        wqk = jnp.concatenate([wl[:, 0:qk_w], wl[:, o_dil:o_dil + qk_w], wl[:, o_moba:o_moba + qk_w]],
                              axis=1).astype(BF16)
        wvt = jnp.concatenate([wl[:, qk_w:q0], wl[:, o_dil + qk_w:o_dil + q0],
                               wl[:, o_moba + qk_w:o_moba + q0]], axis=1).T.astype(BF16)
        kr_pad = jnp.zeros((d, HEAD_PAD), F32).at[:, MLA_NOPE_DIM:MLA_NOPE_DIM + MLA_ROPE_DIM].set(
            wl[:, o_kr:o_gate])
        wsmall = jnp.concatenate([wl[:, o_cq:o_kr], kr_pad], axis=1).astype(BF16)
        wf = jnp.zeros((8, d), F32).at[:N_HEADS].set(wl[:, o_f:o_dil].T)
        wgate = wl[:, o_gate:].astype(BF16)
        bf = jnp.zeros((1, HEAD_PAD), F32).at[0, :N_HEADS].set(b_forget[l])
        per_head = MLA_NOPE_DIM + MLA_ROPE_DIM
        wuqt = _pad_heads(w_uq[l], per_head, 0, per_head, 0).T.astype(BF16)
        wuk = _pad_heads(w_uk[l], MLA_NOPE_DIM, 0, MLA_NOPE_DIM, 0).astype(BF16)

        qk, vt, small, fpre = _inproj(x2, vec(g_pre_mix[l]), wqk, wvt, wsmall, wf, rope_p)
        qk3 = qk.reshape(b, s, 6 * BRANCH_WIDTH)
        fq, fk = _prep(qk3, 0, "fox", [fpre.reshape(b, s, HEAD_PAD)], [bf] + fox_consts)
        dq, dk = _prep(qk3, 2, "dil", [], [place_qt, place_k])
        mq, mk = _prep(qk3, 4, "moba", [], moba_consts)
        lq, lk, lvt = _mla_prep(small, vec(g_cq[l]), vec(g_ckv[l]), wuqt, wuk, w_uv[l].T.astype(BF16),
                                rope_m)
        y_fox = _attention(fq, fk, vt, 0, causal_bias, False)
        y_dil = _attention(dq, dk, vt, 1, dil_bias, True)
        y_moba = _attention(mq, mk, vt, 2, causal_bias, False)
        y_mla = _attention(lq, lk.reshape(b, s, QK_PAD), lvt, 0, causal_bias, False)
        ys = [y.reshape(m, BRANCH_WIDTH) for y in (y_fox, y_dil, y_moba, y_mla)]
        x2 = _merge(x2, vec(g_pre_mix[l]), wgate, ys, w_branch[l].astype(BF16), w_out[l].astype(BF16),
                    vec(g_post_mix[l]))
        x2 = _mlp(x2, vec(g_pre_mlp[l]), w_up[l].astype(BF16), w_down[l].astype(BF16),
                  vec(g_post_mlp[l]))
    return x2.reshape(b, s, d)
```

```python
import functools

import numpy as np
import jax
import jax.numpy as jnp
from jax import lax
from jax.experimental import pallas as pl
from jax.experimental.pallas import tpu as pltpu

D_MODEL = 1024
SEQ = 2048
HEAD_DIM = 64
N_HEADS = 4
BRANCH_WIDTH = N_HEADS * HEAD_DIM
HEAD_PAD = 128
QK_PAD = N_HEADS * HEAD_PAD
BIAS_SLOTS = 16
ROT_DIM = HEAD_DIM // 4
ROPE_THETA = 500000.0
DILATED_PAIRS = ((128, 1), (512, 4), (2048, 16))
MOBA_BLOCK = 256
MOBA_TOPK = 3
N_MOBA_BLOCKS = SEQ // MOBA_BLOCK
MLA_Q_RANK = 256
MLA_KV_RANK = 128
MLA_NOPE_DIM = 64
MLA_ROPE_DIM = 32
MLA_QK_DIM = MLA_NOPE_DIM + MLA_ROPE_DIM
D_FF = 4 * D_MODEL
NORM_EPS = 1e-6
NEG = -1e30
LOG2E = 1.4426950408889634

TOKEN_TILE = 512
ATTN_TILE = 256
ACC_ROWS = HEAD_DIM + 16
FF_CHUNK = 512
VMEM_LIMIT = 56 * 1024 * 1024

F32 = jnp.float32
BF16 = jnp.bfloat16


def _dot(a, b):
    return jnp.dot(a, b, preferred_element_type=F32)


def _dot_nt(a, b):
    return lax.dot_general(a, b, (((1,), (1,)), ((), ())), preferred_element_type=F32)


def _rms(x, g):
    return x * lax.rsqrt(jnp.mean(x * x, axis=-1, keepdims=True) + NORM_EPS) * g


def _rope(y, a, bm, bp, half, axis):
    n = y.shape[axis]
    return y * a + pltpu.roll(y, n - half, axis) * bm + pltpu.roll(y, half, axis) * bp


def _split3(x):
    hi = x.astype(BF16)
    r = x - hi.astype(F32)
    mid = r.astype(BF16)
    lo = (r - mid.astype(F32)).astype(BF16)
    return hi, mid, lo


def _rope_tables():
    def tables(dim):
        inv_freq = 1.0 / (ROPE_THETA ** (jnp.arange(0, dim, 2, dtype=F32) / dim))
        ang = jnp.arange(SEQ, dtype=F32)[:, None] * inv_freq[None, :]
        return jnp.cos(ang), jnp.sin(ang)

    def expand(cos, sin, width, period, start, half):
        pos = np.arange(width) % period - start
        first = (pos >= 0) & (pos < half)
        second = (pos >= half) & (pos < 2 * half)
        idx = np.where(first, pos, np.where(second, pos - half, 0))
        a = jnp.where((first | second)[None, :], cos[:, idx], 1.0)
        bm = jnp.where(first[None, :], -sin[:, idx], 0.0)
        bp = jnp.where(second[None, :], sin[:, idx], 0.0)
        return a, bm, bp

    cos_p, sin_p = tables(ROT_DIM)
    cos_m, sin_m = tables(MLA_ROPE_DIM)
    partial = expand(cos_p, sin_p, BRANCH_WIDTH, HEAD_DIM, 0, ROT_DIM // 2)
    mla_k = expand(cos_m, sin_m, HEAD_PAD, HEAD_PAD, MLA_NOPE_DIM, MLA_ROPE_DIM // 2)
    mla_q = expand(cos_m, sin_m, MLA_QK_DIM, MLA_QK_DIM, MLA_NOPE_DIM, MLA_ROPE_DIM // 2)
    return partial, mla_k, mla_q


def _fox_constants():
    to_k = np.zeros((HEAD_PAD, HEAD_PAD), np.float32)
    to_q = np.zeros((N_HEADS * BIAS_SLOTS, HEAD_PAD), np.float32)
    q_ones = np.zeros((N_HEADS * BIAS_SLOTS, 1), np.float32)
    k_ones = np.zeros((1, HEAD_PAD), np.float32)
    for h in range(N_HEADS):
        for j in range(3):
            to_k[4 * j + h, HEAD_DIM + BIAS_SLOTS * h + 3 + j] = 1.0
            to_q[BIAS_SLOTS * h + j, 4 * j + h] = 1.0
        q_ones[BIAS_SLOTS * h + 3:BIAS_SLOTS * h + 6] = 1.0
        k_ones[0, HEAD_DIM + BIAS_SLOTS * h:HEAD_DIM + BIAS_SLOTS * h + 3] = 1.0
    return [jnp.asarray(to_k, BF16), jnp.asarray(to_q, BF16), jnp.asarray(q_ones), jnp.asarray(k_ones)]


def _moba_key_bias():
    out = np.zeros((1, SEQ, HEAD_PAD), np.float32)
    blk = np.arange(SEQ) // MOBA_BLOCK
    for h in range(N_HEADS):
        for n in range(N_MOBA_BLOCKS):
            out[0, :, HEAD_DIM + BIAS_SLOTS * h + n] = blk == n
    return jnp.asarray(out, BF16)


def _block_indicator():
    ind = np.zeros((HEAD_PAD, SEQ), np.float32)
    for h in range(N_HEADS):
        for n in range(N_MOBA_BLOCKS):
            ind[N_MOBA_BLOCKS * h + n, n * MOBA_BLOCK:(n + 1) * MOBA_BLOCK] = 1.0
    return jnp.asarray(ind, BF16)


def _block_head_mask():
    m = np.zeros((HEAD_PAD, QK_PAD), np.float32)
    for h in range(N_HEADS):
        m[N_MOBA_BLOCKS * h:N_MOBA_BLOCKS * (h + 1), h * HEAD_PAD:h * HEAD_PAD + HEAD_DIM] = 1.0
    return jnp.asarray(m)


def _causal_bias():
    key = np.arange(ATTN_TILE)[:, None]
    query = np.arange(ATTN_TILE)[None, :]
    return jnp.asarray(np.where(key <= query, 0.0, NEG).astype(np.float32)[None])


def _dilated_bias():
    r = np.arange(ATTN_TILE)[None, :]
    c = np.arange(ATTN_TILE)[:, None]
    count = np.zeros((SEQ // ATTN_TILE, ATTN_TILE, ATTN_TILE), np.float32)
    for delta in range(SEQ // ATTN_TILE):
        d = delta * ATTN_TILE + r - c
        for window, dil in DILATED_PAIRS:
            count[delta] += (d >= 0) & (d <= window) & (d % dil == 0)
    return jnp.asarray(np.where(count > 0, np.log2(np.maximum(count, 1.0)), NEG).astype(np.float32))


def _full(shape):
    return pl.BlockSpec(shape, lambda *_: (0,) * len(shape))


def _params(*sem):
    return pltpu.CompilerParams(dimension_semantics=sem, vmem_limit_bytes=VMEM_LIMIT)


def _inproj_kernel(x_ref, g_ref, wk_ref, wt_ref, wsmall_ref, wf_ref, ra_ref, rbm_ref, rbp_ref,
                   rat_ref, rbmt_ref, rbpt_ref, kpad_ref, qvt_ref, small_ref, f_ref):
    hf = _rms(x_ref[...], g_ref[...])
    h = hf.astype(BF16)
    tm = h.shape[0]

    low_half = lax.broadcasted_iota(jnp.int32, (tm, HEAD_PAD), 1) < HEAD_DIM
    for c in range(3):
        y = _dot(h, wk_ref[:, c * BRANCH_WIDTH:(c + 1) * BRANCH_WIDTH])
        if c >= 1:
            y = _rope(y, ra_ref[...], rbm_ref[...], rbp_ref[...], ROT_DIM // 2, 1)
        for pair in range(2):
            both = y[:, pair * HEAD_PAD:(pair + 1) * HEAD_PAD]
            for odd, src in enumerate((both, pltpu.roll(both, HEAD_DIM, 1))):
                col = c * QK_PAD + (2 * pair + odd) * HEAD_PAD
                kpad_ref[:, col:col + HEAD_PAD] = jnp.where(low_half, src, 0.0).astype(BF16)

    t_all = _dot_nt(wt_ref[...], h)
    for c in range(6):
        rows = slice(c * BRANCH_WIDTH, (c + 1) * BRANCH_WIDTH)
        y = t_all[rows, :]
        if c >= 4:
            y = _rope(y, rat_ref[...], rbmt_ref[...], rbpt_ref[...], ROT_DIM // 2, 0)
        if c >= 3:
            y = y * (HEAD_DIM ** -0.5 * LOG2E)
        qvt_ref[0, rows, :] = y.astype(BF16)

    small_ref[...] = _dot(h, wsmall_ref[...])
    lane = lax.broadcasted_iota(jnp.int32, f_ref.shape, 1)
    f = jnp.zeros(f_ref.shape, F32)
    for j in range(N_HEADS):
        col = jnp.sum(hf * wf_ref[j:j + 1, :], axis=-1, keepdims=True)
        f = jnp.where(lane == j, col, f)
    f_ref[...] = f


def _inproj(x2, g, wk, wt, wsmall, wf, rope_p):
    m = x2.shape[0]
    tm = TOKEN_TILE
    seq_tiles = SEQ // tm
    row = lambda i: (i, 0)
    tab = pl.BlockSpec((tm, BRANCH_WIDTH), lambda i: (i % seq_tiles, 0))
    tab_t = pl.BlockSpec((BRANCH_WIDTH, tm), lambda i: (0, i % seq_tiles))
    return pl.pallas_call(
        _inproj_kernel,
        grid=(m // tm,),
        in_specs=[pl.BlockSpec((tm, D_MODEL), row), _full((1, D_MODEL)), _full(wk.shape), _full(wt.shape),
                  _full(wsmall.shape), _full(wf.shape), tab, tab, tab, tab_t, tab_t, tab_t],
        out_specs=[pl.BlockSpec((tm, 3 * QK_PAD), row),
                   pl.BlockSpec((1, 6 * BRANCH_WIDTH, tm), lambda i: (i // seq_tiles, 0, i % seq_tiles)),
                   pl.BlockSpec((tm, 512), row), pl.BlockSpec((tm, HEAD_PAD), row)],
        out_shape=[jax.ShapeDtypeStruct((m, 3 * QK_PAD), BF16),
                   jax.ShapeDtypeStruct((m // SEQ, 6 * BRANCH_WIDTH, SEQ), BF16),
                   jax.ShapeDtypeStruct((m, 512), F32),
                   jax.ShapeDtypeStruct((m, HEAD_PAD), F32)],
        compiler_params=_params("parallel"),
        name="inproj",
    )(x2, g, wk, wt, wsmall, wf, *rope_p, *[t.T for t in rope_p])


def _fox_bias_kernel(f_ref, bf_ref, to_k_ref, to_q_ref, q_ones_ref, k_ones_ref, qb_ref, kb_ref):
    chunk = 256
    ri = lax.broadcasted_iota(jnp.int32, (chunk, chunk), 0)
    ci = lax.broadcasted_iota(jnp.int32, (chunk, chunk), 1)
    tril = jnp.where(ri >= ci, 1.0, 0.0).astype(BF16)
    lane = lax.broadcasted_iota(jnp.int32, (chunk, HEAD_PAD), 1)
    carry = jnp.zeros((1, HEAD_PAD), F32)
    for c in range(SEQ // chunk):
        rows = slice(c * chunk, (c + 1) * chunk)
        z = f_ref[0, rows, :] + bf_ref[...]
        logf = jnp.minimum(z, 0.0) - jnp.log(1.0 + jnp.exp(-jnp.abs(z)))
        hi, mid, lo = _split3(logf)
        cum = _dot(tril, hi) + _dot(tril, mid) + _dot(tril, lo) + carry
        carry = cum[chunk - 1:chunk, :]
        c_hi, c_mid, c_lo = (p.astype(F32) for p in _split3(cum * LOG2E))
        packed = jnp.where(lane < N_HEADS, c_hi,
                           jnp.where(lane < 2 * N_HEADS, pltpu.roll(c_mid, N_HEADS, 1),
                                     pltpu.roll(c_lo, 2 * N_HEADS, 1))).astype(BF16)
        kb_ref[0, rows, :] = (k_ones_ref[...] - _dot(packed, to_k_ref[...])).astype(BF16)
        qb_ref[0, :, rows] = (_dot_nt(to_q_ref[...], packed) + q_ones_ref[...]).astype(BF16)


def _fox_bias(fpre3, bf, consts):
    b = fpre3.shape[0]
    return pl.pallas_call(
        _fox_bias_kernel,
        grid=(b,),
        in_specs=[pl.BlockSpec((1, SEQ, HEAD_PAD), lambda i: (i, 0, 0)), _full(bf.shape)]
        + [_full(c.shape) for c in consts],
        out_specs=[pl.BlockSpec((1, N_HEADS * BIAS_SLOTS, SEQ), lambda i: (i, 0, 0)),
                   pl.BlockSpec((1, SEQ, HEAD_PAD), lambda i: (i, 0, 0))],
        out_shape=[jax.ShapeDtypeStruct((b, N_HEADS * BIAS_SLOTS, SEQ), BF16),
                   jax.ShapeDtypeStruct((b, SEQ, HEAD_PAD), BF16)],
        compiler_params=_params("parallel"),
        name="fox_bias",
    )(fpre3, bf, *consts)


def _moba_bias_kernel(qt_ref, k_ref, ind_ref, hmask_ref, qb_ref):
    kmean = _dot(ind_ref[...], k_ref[0]) * (1.0 / MOBA_BLOCK) * hmask_ref[...]
    pieces = _split3(kmean)
    zeros = jnp.zeros((HEAD_PAD - HEAD_DIM, SEQ), BF16)
    qt_pad = jnp.concatenate(
        [part for h in range(N_HEADS) for part in (qt_ref[0, h * HEAD_DIM:(h + 1) * HEAD_DIM, :], zeros)], axis=0)
    gate = _dot(pieces[0], qt_pad) + _dot(pieces[1], qt_pad) + _dot(pieces[2], qt_pad)
    blk = lax.broadcasted_iota(jnp.int32, (N_MOBA_BLOCKS, SEQ), 0)
    own = lax.broadcasted_iota(jnp.int32, (N_MOBA_BLOCKS, SEQ), 1) // MOBA_BLOCK
    for h in range(N_HEADS):
        g = gate[h * N_MOBA_BLOCKS:(h + 1) * N_MOBA_BLOCKS, :]
        rank = jnp.zeros(g.shape, F32)
        for r in range(1, N_MOBA_BLOCKS):
            other = pltpu.roll(g, N_MOBA_BLOCKS - r, 0)
            other_blk = jnp.where(blk + r >= N_MOBA_BLOCKS, blk + r - N_MOBA_BLOCKS, blk + r)
            ahead = (other > g) | ((other == g) & (other_blk < blk))
            rank = rank + jnp.where(ahead & (other_blk < own), 1.0, 0.0)
        keep = ((blk < own) & (rank < MOBA_TOPK)) | (blk == own)
        bias = jnp.concatenate([jnp.where(keep, 0.0, NEG), jnp.zeros(g.shape, F32)], axis=0)
        qb_ref[0, h * BIAS_SLOTS:(h + 1) * BIAS_SLOTS, :] = bias.astype(BF16)


def _moba_bias(qvt, q_row, kpad3, k_col, ind, hmask):
    b = qvt.shape[0]
    return pl.pallas_call(
        _moba_bias_kernel,
        grid=(b,),
        in_specs=[pl.BlockSpec((1, BRANCH_WIDTH, SEQ), lambda i: (i, q_row, 0)),
                  pl.BlockSpec((1, SEQ, QK_PAD), lambda i: (i, 0, k_col)),
                  _full(ind.shape), _full(hmask.shape)],
        out_specs=pl.BlockSpec((1, N_HEADS * BIAS_SLOTS, SEQ), lambda i: (i, 0, 0)),
        out_shape=jax.ShapeDtypeStruct((b, N_HEADS * BIAS_SLOTS, SEQ), BF16),
        compiler_params=_params("parallel"),
        name="moba_bias",
    )(qvt, kpad3, ind, hmask)


def _mla_kernel(small_ref, gq_ref, gkv_ref, wuqt_ref, wuk_ref, wuvt_ref, ra_ref, rbm_ref, rbp_ref,
                rat_ref, rbmt_ref, rbpt_ref, qt_ref, k_ref, vt_ref):
    half = MLA_ROPE_DIM // 2
    scale = MLA_QK_DIM ** -0.5 * LOG2E
    cq = _rms(small_ref[:, 0:MLA_Q_RANK], gq_ref[...]).astype(BF16)
    ckv = _rms(small_ref[:, MLA_Q_RANK:MLA_Q_RANK + MLA_KV_RANK], gkv_ref[...]).astype(BF16)
    kr = _rope(small_ref[:, MLA_Q_RANK + MLA_KV_RANK:], ra_ref[...], rbm_ref[...], rbp_ref[...],
               half, 1)
    qf_t = _dot_nt(wuqt_ref[...], cq)
    kf = _dot(ckv, wuk_ref[...])
    at, bmt, bpt = rat_ref[...], rbmt_ref[...], rbpt_ref[...]
    for h in range(N_HEADS):
        rows = slice(h * MLA_QK_DIM, (h + 1) * MLA_QK_DIM)
        cols = slice(h * HEAD_PAD, (h + 1) * HEAD_PAD)
        qt_ref[0, rows, :] = (_rope(qf_t[rows, :], at, bmt, bpt, half, 0) * scale).astype(BF16)
        k_ref[:, cols] = (kf[:, cols] + kr).astype(BF16)
    vt_ref[0] = _dot_nt(wuvt_ref[...], ckv).astype(BF16)


def _mla_prep(small, gq, gkv, wuqt, wuk, wuvt, rope_k, rope_q):
    m = small.shape[0]
    tm = TOKEN_TILE
    seq_tiles = SEQ // tm
    row = lambda i: (i, 0)
    by_batch = lambda i: (i // seq_tiles, 0, i % seq_tiles)
    tab = pl.BlockSpec((tm, HEAD_PAD), lambda i: (i % seq_tiles, 0))
    tab_t = pl.BlockSpec((MLA_QK_DIM, tm), lambda i: (0, i % seq_tiles))
    return pl.pallas_call(
        _mla_kernel,
        grid=(m // tm,),
        in_specs=[pl.BlockSpec((tm, 512), row), _full(gq.shape), _full(gkv.shape),
                  _full(wuqt.shape), _full(wuk.shape), _full(wuvt.shape), tab, tab, tab, tab_t, tab_t, tab_t],
        out_specs=[pl.BlockSpec((1, N_HEADS * MLA_QK_DIM, tm), by_batch), pl.BlockSpec((tm, QK_PAD), row),
                   pl.BlockSpec((1, BRANCH_WIDTH, tm), by_batch)],
        out_shape=[jax.ShapeDtypeStruct((m // SEQ, N_HEADS * MLA_QK_DIM, SEQ), BF16),
                   jax.ShapeDtypeStruct((m, QK_PAD), BF16),
                   jax.ShapeDtypeStruct((m // SEQ, BRANCH_WIDTH, SEQ), BF16)],
        compiler_params=_params("parallel"),
        name="mla_prep",
    )(small, gq, gkv, wuqt, wuk, wuvt, *rope_k, *[t.T for t in rope_q])


def _attn_kernel(*refs, q_dim, biased, dilated):
    if biased:
        qt_ref, qb_ref, k_ref, kb_ref, vt_ref, tab_ref, o_ref, sa_ref, sb_ref, m_ref, acc_ref = refs
    else:
        qt_ref, k_ref, vt_ref, tab_ref, o_ref, sa_ref, sb_ref, m_ref, acc_ref = refs
    t = ATTN_TILE
    n_tiles = SEQ // t
    ones_rows = jnp.ones((ACC_ROWS - HEAD_DIM, t), BF16)

    m_ref[...] = jnp.full(m_ref.shape, NEG, F32)
    acc_ref[...] = jnp.zeros(acc_ref.shape, F32)

    def scores(i, j, s_ref):
        qo = pl.multiple_of(i * t, t)
        ko = pl.multiple_of(j * t, t)
        for h in range(N_HEADS):
            parts = [qt_ref[0, h * q_dim:(h + 1) * q_dim, pl.ds(qo, t)]]
            k_op = k_ref[0, pl.ds(ko, t), h * HEAD_PAD:(h + 1) * HEAD_PAD]
            if biased:
                k_op = k_op + kb_ref[0, pl.ds(ko, t), :]
                if h > 0:
                    parts.append(jnp.zeros((h * BIAS_SLOTS, t), BF16))
                parts.append(qb_ref[0, h * BIAS_SLOTS:(h + 1) * BIAS_SLOTS, pl.ds(qo, t)])
            fill = HEAD_PAD - sum(p.shape[0] for p in parts)
            if fill:
                parts.append(jnp.zeros((fill, t), BF16))
            s_ref[h] = _dot(k_op, jnp.concatenate(parts, axis=0))

    def update(i, j, s_ref, bias):
        qo = pl.multiple_of(i * t, t)
        ko = pl.multiple_of(j * t, t)
        for h in range(N_HEADS):
            s = s_ref[h] if bias is None else s_ref[h] + bias
            m_old = m_ref[h, :, pl.ds(qo, t)]
            m_new = jnp.maximum(m_old, jnp.max(s, axis=0, keepdims=True))
            m_ref[h, :, pl.ds(qo, t)] = m_new
            p = jnp.exp2(s - m_new).astype(BF16)
            v_aug = jnp.concatenate([vt_ref[0, h * HEAD_DIM:(h + 1) * HEAD_DIM, pl.ds(ko, t)], ones_rows], axis=0)
            rows = slice(h * ACC_ROWS, (h + 1) * ACC_ROWS)
            acc_ref[rows, pl.ds(qo, t)] = (acc_ref[rows, pl.ds(qo, t)] * jnp.exp2(m_old - m_new)
                                           + _dot(v_aug, p))

    def pipeline(first, succ, n_steps, bias_of):
        def two_steps(_, ij):
            i0, j0 = ij
            i1, j1 = succ(i0, j0)
            i2, j2 = succ(i1, j1)
            scores(i1, j1, sb_ref)
            update(i0, j0, sa_ref, bias_of(i0, j0))
            scores(i2, j2, sa_ref)
            update(i1, j1, sb_ref, bias_of(i1, j1))
            return i2, j2

        scores(*first, sa_ref)
        lax.fori_loop(0, n_steps // 2, two_steps, (jnp.int32(first[0]), jnp.int32(first[1])))

    def succ_diag(i, j):
        nxt = jnp.minimum(i + 1, n_tiles - 1)
        return nxt, nxt

    def succ_below(i, j):
        last = j == i - 1
        return jnp.where(last, jnp.minimum(i + 1, n_tiles - 1), i), jnp.where(last, 0, j + 1)

    pipeline((0, 0), succ_diag, n_tiles, lambda i, j: tab_ref[0])
    pipeline((1, 0), succ_below, n_tiles * (n_tiles - 1) // 2,
             (lambda i, j: tab_ref[i - j]) if dilated else (lambda i, j: None))

    for i in range(n_tiles):
        q_cols = slice(i * t, (i + 1) * t)
        heads = []
        for h in range(N_HEADS):
            denom = acc_ref[h * ACC_ROWS + HEAD_DIM:h * ACC_ROWS + HEAD_DIM + 1, q_cols]
            heads.append(acc_ref[h * ACC_ROWS:h * ACC_ROWS + HEAD_DIM, q_cols] * (1.0 / denom))
        o_ref[0, q_cols, :] = jnp.concatenate(heads, axis=0).T.astype(o_ref.dtype)


def _attention(qt, q_row, q_dim, k, k_col, vt, v_row, table, dilated=False, q_bias=None, k_bias=None):
    b = qt.shape[0]
    t = ATTN_TILE
    biased = q_bias is not None
    in_specs = [pl.BlockSpec((1, N_HEADS * q_dim, SEQ), lambda bi: (bi, q_row, 0))]
    args = [qt]
    if biased:
        in_specs.append(pl.BlockSpec((1, N_HEADS * BIAS_SLOTS, SEQ), lambda bi: (bi, 0, 0)))
        args.append(q_bias)
    in_specs.append(pl.BlockSpec((1, SEQ, QK_PAD), lambda bi: (bi, 0, k_col)))
    args.append(k)
    if biased:
        per_batch = k_bias.shape[0] > 1
        in_specs.append(pl.BlockSpec((1, SEQ, HEAD_PAD), lambda bi: (bi if per_batch else 0, 0, 0)))
        args.append(k_bias)
    in_specs += [pl.BlockSpec((1, BRANCH_WIDTH, SEQ), lambda bi: (bi, v_row, 0)), _full(table.shape)]
    args += [vt, table]
    return pl.pallas_call(
        functools.partial(_attn_kernel, q_dim=q_dim, biased=biased, dilated=dilated),
        grid=(b,),
        in_specs=in_specs,
        out_specs=pl.BlockSpec((1, SEQ, BRANCH_WIDTH), lambda bi: (bi, 0, 0)),
        out_shape=jax.ShapeDtypeStruct((b, SEQ, BRANCH_WIDTH), BF16),
        scratch_shapes=[pltpu.VMEM((N_HEADS, t, t), F32), pltpu.VMEM((N_HEADS, t, t), F32),
                        pltpu.VMEM((N_HEADS, 1, SEQ), F32), pltpu.VMEM((N_HEADS * ACC_ROWS, SEQ), F32)],
        compiler_params=_params("parallel"),
        name="attn_dil" if dilated else ("attn_bias" if biased else "attn"),
    )(*args)


def _merge_kernel(x_ref, gpre_ref, wg_ref, y0_ref, y1_ref, y2_ref, y3_ref, wb_ref, wo_ref, gpost_ref,
                  o_ref):
    x = x_ref[...]
    h = _rms(x, gpre_ref[...]).astype(BF16)
    merged = None
    for n, y_ref in enumerate((y0_ref, y1_ref, y2_ref, y3_ref)):
        gate = jax.nn.sigmoid(_dot(h, wg_ref[:, n * D_MODEL:(n + 1) * D_MODEL]))
        term = gate * _dot(y_ref[...], wb_ref[n])
        merged = term if merged is None else merged + term
    mix = _dot(merged.astype(BF16), wo_ref[...])
    o_ref[...] = x + _rms(mix, gpost_ref[...])


def _merge(x2, gpre, wg, ys, wb, wo, gpost):
    m = x2.shape[0]
    tm = TOKEN_TILE
    row = lambda i: (i, 0)
    yspec = pl.BlockSpec((tm, BRANCH_WIDTH), row)
    return pl.pallas_call(
        _merge_kernel,
        grid=(m // tm,),
        in_specs=[pl.BlockSpec((tm, D_MODEL), row), _full(gpre.shape), _full(wg.shape),
                  yspec, yspec, yspec, yspec, _full(wb.shape), _full(wo.shape), _full(gpost.shape)],
        out_specs=pl.BlockSpec((tm, D_MODEL), row),
        out_shape=jax.ShapeDtypeStruct((m, D_MODEL), F32),
        compiler_params=_params("parallel"),
        name="merge",
    )(x2, gpre, wg, *ys, wb, wo, gpost)


def _mlp_kernel(x_ref, gpre_ref, wup_ref, wdown_ref, gpost_ref, o_ref):
    x = x_ref[...]
    h = _rms(x, gpre_ref[...]).astype(BF16)
    acc = None
    for c in range(D_FF // FF_CHUNK):
        cols = slice(c * FF_CHUNK, (c + 1) * FF_CHUNK)
        up = jnp.maximum(_dot(h, wup_ref[:, cols]), 0.0)
        term = _dot((up * up).astype(BF16), wdown_ref[cols, :])
        acc = term if acc is None else acc + term
    o_ref[...] = x + _rms(acc, gpost_ref[...])


def _mlp(x2, gpre, wup, wdown, gpost):
    m = x2.shape[0]
    tm = TOKEN_TILE
    row = lambda i: (i, 0)
    return pl.pallas_call(
        _mlp_kernel,
        grid=(m // tm,),
        in_specs=[pl.BlockSpec((tm, D_MODEL), row), _full(gpre.shape), _full(wup.shape),
                  _full(wdown.shape), _full(gpost.shape)],
        out_specs=pl.BlockSpec((tm, D_MODEL), row),
        out_shape=jax.ShapeDtypeStruct((m, D_MODEL), F32),
        compiler_params=_params("parallel"),
        name="mlp",
    )(x2, gpre, wup, wdown, gpost)


def _pad_heads(w, width):
    out = jnp.zeros((w.shape[0], QK_PAD), w.dtype)
    for h in range(N_HEADS):
        out = out.at[:, h * HEAD_PAD:h * HEAD_PAD + width].set(w[:, h * width:(h + 1) * width])
    return out


def kernel(x, w_in, b_forget, g_cq, g_ckv, w_uq, w_uk, w_uv, w_branch, w_out, w_up, w_down,
           g_pre_mix, g_post_mix, g_pre_mlp, g_post_mlp):
    b, s, d = x.shape
    assert (s, d) == (SEQ, D_MODEL)
    depth = w_in.shape[0]
    m = b * s
    rope_p, rope_mk, rope_mq = _rope_tables()
    fox_consts = _fox_constants()
    moba_ind, moba_hmask, moba_kb = _block_indicator(), _block_head_mask(), _moba_key_bias()
    dil_bias = _dilated_bias()
    causal_bias = _causal_bias()

    bw = BRANCH_WIDTH
    o_f = 3 * bw
    o_dil = o_f + N_HEADS
    o_moba = o_dil + 3 * bw
    o_cq = o_moba + 3 * bw
    o_ckv = o_cq + MLA_Q_RANK
    o_kr = o_ckv + MLA_KV_RANK
    o_gate = o_kr + MLA_ROPE_DIM

    x2 = x.reshape(m, d)
    vec = lambda g: g.reshape(1, -1)
    for l in range(depth):
        wl = w_in[l]
        qkv_at = (0, o_dil, o_moba)
        wk = jnp.concatenate([wl[:, o + bw:o + 2 * bw] for o in qkv_at], axis=1).astype(BF16)
        wt = jnp.concatenate([wl[:, o + 2 * bw:o + 3 * bw] for o in qkv_at]
                             + [wl[:, o:o + bw] for o in qkv_at], axis=1).T.astype(BF16)
        kr_pad = jnp.zeros((d, HEAD_PAD), F32).at[:, MLA_NOPE_DIM:MLA_QK_DIM].set(wl[:, o_kr:o_gate])
        wsmall = jnp.concatenate([wl[:, o_cq:o_kr], kr_pad], axis=1).astype(BF16)
        wf = jnp.zeros((8, d), F32).at[:N_HEADS].set(wl[:, o_f:o_dil].T)
        wgate = wl[:, o_gate:].astype(BF16)
        bf = jnp.zeros((1, HEAD_PAD), F32).at[0, :N_HEADS].set(b_forget[l])
        wuqt = w_uq[l].T.astype(BF16)
        wuk = _pad_heads(w_uk[l], MLA_NOPE_DIM).astype(BF16)

        kpad, qvt, small, fpre = _inproj(x2, vec(g_pre_mix[l]), wk, wt, wsmall, wf, rope_p)
        kpad3 = kpad.reshape(b, s, 3 * QK_PAD)
        fox_qb, fox_kb = _fox_bias(fpre.reshape(b, s, HEAD_PAD), bf, fox_consts)
        moba_qb = _moba_bias(qvt, 5, kpad3, 2, moba_ind, moba_hmask)
        lqt, lk, lvt = _mla_prep(small, vec(g_cq[l]), vec(g_ckv[l]), wuqt, wuk, w_uv[l].T.astype(BF16),
                                 rope_mk, rope_mq)
        y_fox = _attention(qvt, 3, HEAD_DIM, kpad3, 0, qvt, 0, causal_bias, q_bias=fox_qb, k_bias=fox_kb)
        y_dil = _attention(qvt, 4, HEAD_DIM, kpad3, 1, qvt, 1, dil_bias, dilated=True)
        y_moba = _attention(qvt, 5, HEAD_DIM, kpad3, 2, qvt, 2, causal_bias, q_bias=moba_qb, k_bias=moba_kb)
        y_mla = _attention(lqt, 0, MLA_QK_DIM, lk.reshape(b, s, QK_PAD), 0, lvt, 0, causal_bias)
        ys = [y.reshape(m, BRANCH_WIDTH) for y in (y_fox, y_dil, y_moba, y_mla)]
        x2 = _merge(x2, vec(g_pre_mix[l]), wgate, ys, w_branch[l].astype(BF16), w_out[l].astype(BF16),
                    vec(g_post_mix[l]))
        x2 = _mlp(x2, vec(g_pre_mlp[l]), w_up[l].astype(BF16), w_down[l].astype(BF16),
                  vec(g_post_mlp[l]))
    return x2.reshape(b, s, d)
```

```python
import functools

import numpy as np
import jax
import jax.numpy as jnp
from jax import lax
from jax.experimental import pallas as pl
from jax.experimental.pallas import tpu as pltpu

D_MODEL = 1024
SEQ = 2048
HEAD_DIM = 64
N_HEADS = 4
BRANCH_WIDTH = N_HEADS * HEAD_DIM
HEAD_PAD = 128
QK_PAD = N_HEADS * HEAD_PAD
BIAS_SLOTS = 16
ROT_DIM = HEAD_DIM // 4
ROPE_THETA = 500000.0
DILATED_PAIRS = ((128, 1), (512, 4), (2048, 16))
MOBA_BLOCK = 256
MOBA_TOPK = 3
N_MOBA_BLOCKS = SEQ // MOBA_BLOCK
MLA_Q_RANK = 256
MLA_KV_RANK = 128
MLA_NOPE_DIM = 64
MLA_ROPE_DIM = 32
MLA_QK_DIM = MLA_NOPE_DIM + MLA_ROPE_DIM
D_FF = 4 * D_MODEL
NORM_EPS = 1e-6
NEG = -1e30
LOG2E = 1.4426950408889634

TOKEN_TILE = 512
ATTN_TILE = 256
ACC_ROWS = HEAD_DIM + 16
FF_CHUNK = 512
VMEM_LIMIT = 56 * 1024 * 1024

F32 = jnp.float32
BF16 = jnp.bfloat16


def _dot(a, b):
    return jnp.dot(a, b, preferred_element_type=F32)


def _dot_nt(a, b):
    return lax.dot_general(a, b, (((1,), (1,)), ((), ())), preferred_element_type=F32)


def _rms(x, g):
    return x * lax.rsqrt(jnp.mean(x * x, axis=-1, keepdims=True) + NORM_EPS) * g


def _rope(y, a, bm, bp, half, axis):
    n = y.shape[axis]
    return y * a + pltpu.roll(y, n - half, axis) * bm + pltpu.roll(y, half, axis) * bp


def _split3(x):
    hi = x.astype(BF16)
    r = x - hi.astype(F32)
    mid = r.astype(BF16)
    lo = (r - mid.astype(F32)).astype(BF16)
    return hi, mid, lo


def _rope_tables():
    def tables(dim):
        inv_freq = 1.0 / (ROPE_THETA ** (jnp.arange(0, dim, 2, dtype=F32) / dim))
        ang = jnp.arange(SEQ, dtype=F32)[:, None] * inv_freq[None, :]
        return jnp.cos(ang), jnp.sin(ang)

    def expand(cos, sin, width, period, start, half):
        pos = np.arange(width) % period - start
        first = (pos >= 0) & (pos < half)
        second = (pos >= half) & (pos < 2 * half)
        idx = np.where(first, pos, np.where(second, pos - half, 0))
        a = jnp.where((first | second)[None, :], cos[:, idx], 1.0)
        bm = jnp.where(first[None, :], -sin[:, idx], 0.0)
        bp = jnp.where(second[None, :], sin[:, idx], 0.0)
        return a, bm, bp

    cos_p, sin_p = tables(ROT_DIM)
    cos_m, sin_m = tables(MLA_ROPE_DIM)
    partial = expand(cos_p, sin_p, BRANCH_WIDTH, HEAD_DIM, 0, ROT_DIM // 2)
    mla_k = expand(cos_m, sin_m, HEAD_PAD, HEAD_PAD, MLA_NOPE_DIM, MLA_ROPE_DIM // 2)
    mla_q = expand(cos_m, sin_m, MLA_QK_DIM, MLA_QK_DIM, MLA_NOPE_DIM, MLA_ROPE_DIM // 2)
    return partial, mla_k, mla_q


def _fox_constants():
    to_k = np.zeros((HEAD_PAD, HEAD_PAD), np.float32)
    to_q = np.zeros((N_HEADS * BIAS_SLOTS, HEAD_PAD), np.float32)
    q_ones = np.zeros((N_HEADS * BIAS_SLOTS, 1), np.float32)
    k_ones = np.zeros((1, HEAD_PAD), np.float32)
    for h in range(N_HEADS):
        for j in range(3):
            to_k[4 * j + h, HEAD_DIM + BIAS_SLOTS * h + 3 + j] = 1.0
            to_q[BIAS_SLOTS * h + j, 4 * j + h] = 1.0
        q_ones[BIAS_SLOTS * h + 3:BIAS_SLOTS * h + 6] = 1.0
        k_ones[0, HEAD_DIM + BIAS_SLOTS * h:HEAD_DIM + BIAS_SLOTS * h + 3] = 1.0
    return [jnp.asarray(to_k, BF16), jnp.asarray(to_q, BF16), jnp.asarray(q_ones), jnp.asarray(k_ones)]


def _moba_key_bias():
    out = np.zeros((1, SEQ, HEAD_PAD), np.float32)
    blk = np.arange(SEQ) // MOBA_BLOCK
    for h in range(N_HEADS):
        for n in range(N_MOBA_BLOCKS):
            out[0, :, HEAD_DIM + BIAS_SLOTS * h + n] = blk == n
    return jnp.asarray(out, BF16)


def _block_indicator():
    ind = np.zeros((HEAD_PAD, SEQ), np.float32)
    for h in range(N_HEADS):
        for n in range(N_MOBA_BLOCKS):
            ind[N_MOBA_BLOCKS * h + n, n * MOBA_BLOCK:(n + 1) * MOBA_BLOCK] = 1.0
    return jnp.asarray(ind, BF16)


def _block_head_mask():
    m = np.zeros((HEAD_PAD, QK_PAD), np.float32)
    for h in range(N_HEADS):
        m[N_MOBA_BLOCKS * h:N_MOBA_BLOCKS * (h + 1), h * HEAD_PAD:h * HEAD_PAD + HEAD_DIM] = 1.0
    return jnp.asarray(m)


def _causal_bias():
    key = np.arange(ATTN_TILE)[:, None]
    query = np.arange(ATTN_TILE)[None, :]
    return jnp.asarray(np.where(key <= query, 0.0, NEG).astype(np.float32)[None])


def _dilated_bias():
    r = np.arange(ATTN_TILE)[None, :]
    c = np.arange(ATTN_TILE)[:, None]
    count = np.zeros((SEQ // ATTN_TILE, ATTN_TILE, ATTN_TILE), np.float32)
    for delta in range(SEQ // ATTN_TILE):
        d = delta * ATTN_TILE + r - c
        for window, dil in DILATED_PAIRS:
            count[delta] += (d >= 0) & (d <= window) & (d % dil == 0)
    return jnp.asarray(np.where(count > 0, np.log2(np.maximum(count, 1.0)), NEG).astype(np.float32))


def _full(shape):
    return pl.BlockSpec(shape, lambda *_: (0,) * len(shape), pipeline_mode=pl.Buffered(1))


def _params(*sem):
    return pltpu.CompilerParams(dimension_semantics=sem, vmem_limit_bytes=VMEM_LIMIT)


def _inproj_kernel(x_ref, g_ref, wk_ref, wt_ref, wsmall_ref, wf_ref, gq_ref, gkv_ref, wuqt_ref, wuk_ref,
                   wuvt_ref, ra_ref, rbm_ref, rbp_ref, rat_ref, rbmt_ref, rbpt_ref, mra_ref, mrbm_ref,
                   mrbp_ref, mrat_ref, mrbmt_ref, mrbpt_ref, kpad_ref, qvt_ref, lqt_ref, f_ref):
    hf = _rms(x_ref[...], g_ref[...])
    h = hf.astype(BF16)
    tm = h.shape[0]

    low_half = lax.broadcasted_iota(jnp.int32, (tm, HEAD_PAD), 1) < HEAD_DIM
    for c in range(3):
        y = _dot(h, wk_ref[:, c * BRANCH_WIDTH:(c + 1) * BRANCH_WIDTH])
        if c >= 1:
            y = _rope(y, ra_ref[...], rbm_ref[...], rbp_ref[...], ROT_DIM // 2, 1)
        for pair in range(2):
            both = y[:, pair * HEAD_PAD:(pair + 1) * HEAD_PAD]
            for odd, src in enumerate((both, pltpu.roll(both, HEAD_DIM, 1))):
                col = c * QK_PAD + (2 * pair + odd) * HEAD_PAD
                kpad_ref[:, col:col + HEAD_PAD] = jnp.where(low_half, src, 0.0).astype(BF16)

    t_all = _dot_nt(wt_ref[...], h)
    for c in range(6):
        rows = slice(c * BRANCH_WIDTH, (c + 1) * BRANCH_WIDTH)
        y = t_all[rows, :]
        if c >= 4:
            y = _rope(y, rat_ref[...], rbmt_ref[...], rbpt_ref[...], ROT_DIM // 2, 0)
        if c >= 3:
            y = y * (HEAD_DIM ** -0.5 * LOG2E)
        qvt_ref[0, rows, :] = y.astype(BF16)

    small = _dot(h, wsmall_ref[...])
    half = MLA_ROPE_DIM // 2
    cq = _rms(small[:, 0:MLA_Q_RANK], gq_ref[...]).astype(BF16)
    ckv = _rms(small[:, MLA_Q_RANK:MLA_Q_RANK + MLA_KV_RANK], gkv_ref[...]).astype(BF16)
    kr = _rope(small[:, MLA_Q_RANK + MLA_KV_RANK:], mra_ref[...], mrbm_ref[...], mrbp_ref[...], half, 1)
    qf_t = _dot_nt(wuqt_ref[...], cq)
    kf = _dot(ckv, wuk_ref[...])
    for hd in range(N_HEADS):
        rows = slice(hd * MLA_QK_DIM, (hd + 1) * MLA_QK_DIM)
        q_h = _rope(qf_t[rows, :], mrat_ref[...], mrbmt_ref[...], mrbpt_ref[...], half, 0)
        lqt_ref[0, rows, :] = (q_h * (MLA_QK_DIM ** -0.5 * LOG2E)).astype(BF16)
        col = 3 * QK_PAD + hd * HEAD_PAD
        kpad_ref[:, col:col + HEAD_PAD] = (kf[:, hd * HEAD_PAD:(hd + 1) * HEAD_PAD] + kr).astype(BF16)
    qvt_ref[0, 6 * BRANCH_WIDTH:, :] = _dot_nt(wuvt_ref[...], ckv).astype(BF16)

    lane = lax.broadcasted_iota(jnp.int32, f_ref.shape, 1)
    f = jnp.zeros(f_ref.shape, F32)
    for j in range(N_HEADS):
        col = jnp.sum(hf * wf_ref[j:j + 1, :], axis=-1, keepdims=True)
        f = jnp.where(lane == j, col, f)
    f_ref[...] = f


def _inproj(x2, g, weights, rope_p, rope_mk, rope_mq):
    m = x2.shape[0]
    tm = TOKEN_TILE
    seq_tiles = SEQ // tm
    row = lambda i: (i, 0)
    by_batch = lambda i: (i // seq_tiles, 0, i % seq_tiles)
    seq_rows = lambda width: pl.BlockSpec((tm, width), lambda i: (i % seq_tiles, 0))
    seq_cols = lambda height: pl.BlockSpec((height, tm), lambda i: (0, i % seq_tiles))
    tables = ([seq_rows(BRANCH_WIDTH)] * 3 + [seq_cols(BRANCH_WIDTH)] * 3
              + [seq_rows(HEAD_PAD)] * 3 + [seq_cols(MLA_QK_DIM)] * 3)
    return pl.pallas_call(
        _inproj_kernel,
        grid=(m // tm,),
        in_specs=[pl.BlockSpec((tm, D_MODEL), row), _full((1, D_MODEL))] + [_full(w.shape) for w in weights]
        + tables,
        out_specs=[pl.BlockSpec((tm, 4 * QK_PAD), row), pl.BlockSpec((1, 7 * BRANCH_WIDTH, tm), by_batch),
                   pl.BlockSpec((1, N_HEADS * MLA_QK_DIM, tm), by_batch), pl.BlockSpec((tm, HEAD_PAD), row)],
        out_shape=[jax.ShapeDtypeStruct((m, 4 * QK_PAD), BF16),
                   jax.ShapeDtypeStruct((m // SEQ, 7 * BRANCH_WIDTH, SEQ), BF16),
                   jax.ShapeDtypeStruct((m // SEQ, N_HEADS * MLA_QK_DIM, SEQ), BF16),
                   jax.ShapeDtypeStruct((m, HEAD_PAD), F32)],
        compiler_params=_params("parallel"),
        name="inproj",
    )(x2, g, *weights, *rope_p, *[t.T for t in rope_p], *rope_mk, *[t.T for t in rope_mq])


def _fox_bias_kernel(f_ref, bf_ref, to_k_ref, to_q_ref, q_ones_ref, k_ones_ref, qb_ref, kb_ref):
    chunk = 256
    ri = lax.broadcasted_iota(jnp.int32, (chunk, chunk), 0)
    ci = lax.broadcasted_iota(jnp.int32, (chunk, chunk), 1)
    tril = jnp.where(ri >= ci, 1.0, 0.0).astype(BF16)
    lane = lax.broadcasted_iota(jnp.int32, (chunk, HEAD_PAD), 1)
    carry = jnp.zeros((1, HEAD_PAD), F32)
    for c in range(SEQ // chunk):
        rows = slice(c * chunk, (c + 1) * chunk)
        z = f_ref[0, rows, :] + bf_ref[...]
        logf = jnp.minimum(z, 0.0) - jnp.log(1.0 + jnp.exp(-jnp.abs(z)))
        hi, mid, lo = _split3(logf)
        cum = _dot(tril, hi) + _dot(tril, mid) + _dot(tril, lo) + carry
        carry = cum[chunk - 1:chunk, :]
        c_hi, c_mid, c_lo = (p.astype(F32) for p in _split3(cum * LOG2E))
        packed = jnp.where(lane < N_HEADS, c_hi,
                           jnp.where(lane < 2 * N_HEADS, pltpu.roll(c_mid, N_HEADS, 1),
                                     pltpu.roll(c_lo, 2 * N_HEADS, 1))).astype(BF16)
        kb_ref[0, rows, :] = (k_ones_ref[...] - _dot(packed, to_k_ref[...])).astype(BF16)
        qb_ref[0, :, rows] = (_dot_nt(to_q_ref[...], packed) + q_ones_ref[...]).astype(BF16)


def _fox_bias(fpre3, bf, consts):
    b = fpre3.shape[0]
    return pl.pallas_call(
        _fox_bias_kernel,
        grid=(b,),
        in_specs=[pl.BlockSpec((1, SEQ, HEAD_PAD), lambda i: (i, 0, 0)), _full(bf.shape)]
        + [_full(c.shape) for c in consts],
        out_specs=[pl.BlockSpec((1, N_HEADS * BIAS_SLOTS, SEQ), lambda i: (i, 0, 0)),
                   pl.BlockSpec((1, SEQ, HEAD_PAD), lambda i: (i, 0, 0))],
        out_shape=[jax.ShapeDtypeStruct((b, N_HEADS * BIAS_SLOTS, SEQ), BF16),
                   jax.ShapeDtypeStruct((b, SEQ, HEAD_PAD), BF16)],
        compiler_params=_params("parallel"),
        name="fox_bias",
    )(fpre3, bf, *consts)


def _moba_bias_kernel(qt_ref, k_ref, ind_ref, hmask_ref, qb_ref):
    kmean = _dot(ind_ref[...], k_ref[0]) * (1.0 / MOBA_BLOCK) * hmask_ref[...]
    pieces = _split3(kmean)
    zeros = jnp.zeros((HEAD_PAD - HEAD_DIM, SEQ), BF16)
    qt_pad = jnp.concatenate(
        [part for h in range(N_HEADS) for part in (qt_ref[0, h * HEAD_DIM:(h + 1) * HEAD_DIM, :], zeros)], axis=0)
    gate = _dot(pieces[0], qt_pad) + _dot(pieces[1], qt_pad) + _dot(pieces[2], qt_pad)
    blk = lax.broadcasted_iota(jnp.int32, (N_MOBA_BLOCKS, SEQ), 0)
    own = lax.broadcasted_iota(jnp.int32, (N_MOBA_BLOCKS, SEQ), 1) // MOBA_BLOCK
    for h in range(N_HEADS):
        g = gate[h * N_MOBA_BLOCKS:(h + 1) * N_MOBA_BLOCKS, :]
        rank = jnp.zeros(g.shape, F32)
        for r in range(1, N_MOBA_BLOCKS):
            other = pltpu.roll(g, N_MOBA_BLOCKS - r, 0)
            other_blk = jnp.where(blk + r >= N_MOBA_BLOCKS, blk + r - N_MOBA_BLOCKS, blk + r)
            ahead = (other > g) | ((other == g) & (other_blk < blk))
            rank = rank + jnp.where(ahead & (other_blk < own), 1.0, 0.0)
        keep = ((blk < own) & (rank < MOBA_TOPK)) | (blk == own)
        bias = jnp.concatenate([jnp.where(keep, 0.0, NEG), jnp.zeros(g.shape, F32)], axis=0)
        qb_ref[0, h * BIAS_SLOTS:(h + 1) * BIAS_SLOTS, :] = bias.astype(BF16)


def _moba_bias(qvt, q_row, kpad3, k_col, ind, hmask):
    b = qvt.shape[0]
    return pl.pallas_call(
        _moba_bias_kernel,
        grid=(b,),
        in_specs=[pl.BlockSpec((1, BRANCH_WIDTH, SEQ), lambda i: (i, q_row, 0)),
                  pl.BlockSpec((1, SEQ, QK_PAD), lambda i: (i, 0, k_col)),
                  _full(ind.shape), _full(hmask.shape)],
        out_specs=pl.BlockSpec((1, N_HEADS * BIAS_SLOTS, SEQ), lambda i: (i, 0, 0)),
        out_shape=jax.ShapeDtypeStruct((b, N_HEADS * BIAS_SLOTS, SEQ), BF16),
        compiler_params=_params("parallel"),
        name="moba_bias",
    )(qvt, kpad3, ind, hmask)


def _attn_kernel(*refs, q_dim, biased, dilated):
    if biased:
        qt_ref, qb_ref, k_ref, kb_ref, vt_ref, tab_ref, o_ref, sa_ref, sb_ref, m_ref, acc_ref = refs
    else:
        qt_ref, k_ref, vt_ref, tab_ref, o_ref, sa_ref, sb_ref, m_ref, acc_ref = refs
    t = ATTN_TILE
    n_tiles = SEQ // t
    ones_rows = jnp.ones((ACC_ROWS - HEAD_DIM, t), BF16)

    m_ref[...] = jnp.full(m_ref.shape, NEG, F32)
    acc_ref[...] = jnp.zeros(acc_ref.shape, F32)

    def scores(i, j, s_ref):
        qo = pl.multiple_of(i * t, t)
        ko = pl.multiple_of(j * t, t)
        for h in range(N_HEADS):
            parts = [qt_ref[0, h * q_dim:(h + 1) * q_dim, pl.ds(qo, t)]]
            k_op = k_ref[0, pl.ds(ko, t), h * HEAD_PAD:(h + 1) * HEAD_PAD]
            if biased:
                k_op = k_op + kb_ref[0, pl.ds(ko, t), :]
                if h > 0:
                    parts.append(jnp.zeros((h * BIAS_SLOTS, t), BF16))
                parts.append(qb_ref[0, h * BIAS_SLOTS:(h + 1) * BIAS_SLOTS, pl.ds(qo, t)])
            fill = HEAD_PAD - sum(p.shape[0] for p in parts)
            if fill:
                parts.append(jnp.zeros((fill, t), BF16))
            s_ref[h] = _dot(k_op, jnp.concatenate(parts, axis=0))

    def update(i, j, s_ref, bias):
        qo = pl.multiple_of(i * t, t)
        ko = pl.multiple_of(j * t, t)
        for h in range(N_HEADS):
            s = s_ref[h] if bias is None else s_ref[h] + bias
            m_old = m_ref[h, :, pl.ds(qo, t)]
            m_new = jnp.maximum(m_old, jnp.max(s, axis=0, keepdims=True))
            m_ref[h, :, pl.ds(qo, t)] = m_new
            p = jnp.exp2(s - m_new).astype(BF16)
            v_aug = jnp.concatenate([vt_ref[0, h * HEAD_DIM:(h + 1) * HEAD_DIM, pl.ds(ko, t)], ones_rows], axis=0)
            rows = slice(h * ACC_ROWS, (h + 1) * ACC_ROWS)
            acc_ref[rows, pl.ds(qo, t)] = (acc_ref[rows, pl.ds(qo, t)] * jnp.exp2(m_old - m_new)
                                           + _dot(v_aug, p))

    def pipeline(first, succ, n_steps, bias_of):
        def two_steps(_, ij):
            i0, j0 = ij
            i1, j1 = succ(i0, j0)
            i2, j2 = succ(i1, j1)
            scores(i1, j1, sb_ref)
            update(i0, j0, sa_ref, bias_of(i0, j0))
            scores(i2, j2, sa_ref)
            update(i1, j1, sb_ref, bias_of(i1, j1))
            return i2, j2

        scores(*first, sa_ref)
        lax.fori_loop(0, n_steps // 2, two_steps, (jnp.int32(first[0]), jnp.int32(first[1])))

    def succ_diag(i, j):
        nxt = jnp.minimum(i + 1, n_tiles - 1)
        return nxt, nxt

    def succ_below(i, j):
        last = j == i - 1
        return jnp.where(last, jnp.minimum(i + 1, n_tiles - 1), i), jnp.where(last, 0, j + 1)

    pipeline((0, 0), succ_diag, n_tiles, lambda i, j: tab_ref[0])
    pipeline((1, 0), succ_below, n_tiles * (n_tiles - 1) // 2,
             (lambda i, j: tab_ref[i - j]) if dilated else (lambda i, j: None))

    for i in range(n_tiles):
        q_cols = slice(i * t, (i + 1) * t)
        heads = []
        for h in range(N_HEADS):
            denom = acc_ref[h * ACC_ROWS + HEAD_DIM:h * ACC_ROWS + HEAD_DIM + 1, q_cols]
            heads.append(acc_ref[h * ACC_ROWS:h * ACC_ROWS + HEAD_DIM, q_cols] * (1.0 / denom))
        o_ref[0, q_cols, :] = jnp.concatenate(heads, axis=0).T.astype(o_ref.dtype)


def _attention(qt, q_row, q_dim, k, k_col, vt, v_row, table, dilated=False, q_bias=None, k_bias=None):
    b = qt.shape[0]
    t = ATTN_TILE
    biased = q_bias is not None
    in_specs = [pl.BlockSpec((1, N_HEADS * q_dim, SEQ), lambda bi: (bi, q_row, 0))]
    args = [qt]
    if biased:
        in_specs.append(pl.BlockSpec((1, N_HEADS * BIAS_SLOTS, SEQ), lambda bi: (bi, 0, 0)))
        args.append(q_bias)
    in_specs.append(pl.BlockSpec((1, SEQ, QK_PAD), lambda bi: (bi, 0, k_col)))
    args.append(k)
    if biased:
        per_batch = k_bias.shape[0] > 1
        in_specs.append(pl.BlockSpec((1, SEQ, HEAD_PAD), lambda bi: (bi if per_batch else 0, 0, 0)))
        args.append(k_bias)
    in_specs += [pl.BlockSpec((1, BRANCH_WIDTH, SEQ), lambda bi: (bi, v_row, 0)), _full(table.shape)]
    args += [vt, table]
    return pl.pallas_call(
        functools.partial(_attn_kernel, q_dim=q_dim, biased=biased, dilated=dilated),
        grid=(b,),
        in_specs=in_specs,
        out_specs=pl.BlockSpec((1, SEQ, BRANCH_WIDTH), lambda bi: (bi, 0, 0)),
        out_shape=jax.ShapeDtypeStruct((b, SEQ, BRANCH_WIDTH), BF16),
        scratch_shapes=[pltpu.VMEM((N_HEADS, t, t), F32), pltpu.VMEM((N_HEADS, t, t), F32),
                        pltpu.VMEM((N_HEADS, 1, SEQ), F32), pltpu.VMEM((N_HEADS * ACC_ROWS, SEQ), F32)],
        compiler_params=_params("parallel"),
        name="attn_dil" if dilated else ("attn_bias" if biased else "attn"),
    )(*args)


def _merge_kernel(x_ref, gpre_ref, wg_ref, y0_ref, y1_ref, y2_ref, y3_ref, wb_ref, wo_ref, gpost_ref,
                  o_ref):
    x = x_ref[...]
    h = _rms(x, gpre_ref[...]).astype(BF16)
    merged = None
    for n, y_ref in enumerate((y0_ref, y1_ref, y2_ref, y3_ref)):
        gate = jax.nn.sigmoid(_dot(h, wg_ref[:, n * D_MODEL:(n + 1) * D_MODEL]))
        term = gate * _dot(y_ref[...], wb_ref[n])
        merged = term if merged is None else merged + term
    mix = _dot(merged.astype(BF16), wo_ref[...])
    o_ref[...] = x + _rms(mix, gpost_ref[...])


def _merge(x2, gpre, wg, ys, wb, wo, gpost):
    m = x2.shape[0]
    tm = TOKEN_TILE
    row = lambda i: (i, 0)
    yspec = pl.BlockSpec((tm, BRANCH_WIDTH), row)
    return pl.pallas_call(
        _merge_kernel,
        grid=(m // tm,),
        in_specs=[pl.BlockSpec((tm, D_MODEL), row), _full(gpre.shape), _full(wg.shape),
                  yspec, yspec, yspec, yspec, _full(wb.shape), _full(wo.shape), _full(gpost.shape)],
        out_specs=pl.BlockSpec((tm, D_MODEL), row),
        out_shape=jax.ShapeDtypeStruct((m, D_MODEL), F32),
        compiler_params=_params("parallel"),
        name="merge",
    )(x2, gpre, wg, *ys, wb, wo, gpost)


def _mlp_kernel(x_ref, gpre_ref, wup_ref, wdown_ref, gpost_ref, o_ref):
    x = x_ref[...]
    h = _rms(x, gpre_ref[...]).astype(BF16)
    acc = None
    for c in range(D_FF // FF_CHUNK):
        cols = slice(c * FF_CHUNK, (c + 1) * FF_CHUNK)
        up = jnp.maximum(_dot(h, wup_ref[:, cols]), 0.0)
        term = _dot((up * up).astype(BF16), wdown_ref[cols, :])
        acc = term if acc is None else acc + term
    o_ref[...] = x + _rms(acc, gpost_ref[...])


def _mlp(x2, gpre, wup, wdown, gpost):
    m = x2.shape[0]
    tm = TOKEN_TILE
    row = lambda i: (i, 0)
    return pl.pallas_call(
        _mlp_kernel,
        grid=(m // tm,),
        in_specs=[pl.BlockSpec((tm, D_MODEL), row), _full(gpre.shape), _full(wup.shape),
                  _full(wdown.shape), _full(gpost.shape)],
        out_specs=pl.BlockSpec((tm, D_MODEL), row),
        out_shape=jax.ShapeDtypeStruct((m, D_MODEL), F32),
        compiler_params=_params("parallel"),
        name="mlp",
    )(x2, gpre, wup, wdown, gpost)


def kernel(x, w_in, b_forget, g_cq, g_ckv, w_uq, w_uk, w_uv, w_branch, w_out, w_up, w_down,
           g_pre_mix, g_post_mix, g_pre_mlp, g_post_mlp):
    b, s, d = x.shape
    assert (s, d) == (SEQ, D_MODEL)
    depth = w_in.shape[0]
    m = b * s
    rope_p, rope_mk, rope_mq = _rope_tables()
    fox_consts = _fox_constants()
    moba_ind, moba_hmask, moba_kb = _block_indicator(), _block_head_mask(), _moba_key_bias()
    dil_bias = _dilated_bias()
    causal_bias = _causal_bias()

    bw = BRANCH_WIDTH
    o_f = 3 * bw
    o_dil = o_f + N_HEADS
    o_moba = o_dil + 3 * bw
    o_cq = o_moba + 3 * bw
    o_ckv = o_cq + MLA_Q_RANK
    o_kr = o_ckv + MLA_KV_RANK
    o_gate = o_kr + MLA_ROPE_DIM

    qkv_at = (0, o_dil, o_moba)
    wk_all = jnp.concatenate([w_in[:, :, o + bw:o + 2 * bw] for o in qkv_at], axis=2).astype(BF16)
    wt_all = jnp.swapaxes(jnp.concatenate([w_in[:, :, o + 2 * bw:o + 3 * bw] for o in qkv_at]
                                          + [w_in[:, :, o:o + bw] for o in qkv_at], axis=2), 1, 2).astype(BF16)
    wsmall_all = jnp.concatenate(
        [w_in[:, :, o_cq:o_kr], jnp.zeros((depth, d, MLA_NOPE_DIM), F32), w_in[:, :, o_kr:o_gate],
         jnp.zeros((depth, d, HEAD_PAD - MLA_QK_DIM), F32)], axis=2).astype(BF16)
    wf_all = jnp.concatenate([jnp.swapaxes(w_in[:, :, o_f:o_dil], 1, 2),
                              jnp.zeros((depth, 8 - N_HEADS, d), F32)], axis=1)
    wgate_all = w_in[:, :, o_gate:].astype(BF16)
    bf_all = jnp.concatenate([b_forget, jnp.zeros((depth, HEAD_PAD - N_HEADS), F32)], axis=1)
    wuqt_all = jnp.swapaxes(w_uq, 1, 2).astype(BF16)
    wuk_all = jnp.concatenate(
        [part for h in range(N_HEADS) for part in (w_uk[:, :, h * MLA_NOPE_DIM:(h + 1) * MLA_NOPE_DIM],
                                                   jnp.zeros((depth, MLA_KV_RANK, HEAD_PAD - MLA_NOPE_DIM), F32))],
        axis=2).astype(BF16)
    wuvt_all = jnp.swapaxes(w_uv, 1, 2).astype(BF16)
    wb_all, wo_all = w_branch.astype(BF16), w_out.astype(BF16)
    wup_all, wdown_all = w_up.astype(BF16), w_down.astype(BF16)

    x2 = x.reshape(m, d)
    vec = lambda g: g.reshape(1, -1)
    for l in range(depth):
        weights = [wk_all[l], wt_all[l], wsmall_all[l], wf_all[l], vec(g_cq[l]), vec(g_ckv[l]),
                   wuqt_all[l], wuk_all[l], wuvt_all[l]]
        kpad, qvt, lqt, fpre = _inproj(x2, vec(g_pre_mix[l]), weights, rope_p, rope_mk, rope_mq)
        kpad3 = kpad.reshape(b, s, 4 * QK_PAD)
        fox_qb, fox_kb = _fox_bias(fpre.reshape(b, s, HEAD_PAD), bf_all[l:l + 1], fox_consts)
        moba_qb = _moba_bias(qvt, 5, kpad3, 2, moba_ind, moba_hmask)
        y_fox = _attention(qvt, 3, HEAD_DIM, kpad3, 0, qvt, 0, causal_bias, q_bias=fox_qb, k_bias=fox_kb)
        y_dil = _attention(qvt, 4, HEAD_DIM, kpad3, 1, qvt, 1, dil_bias, dilated=True)
        y_moba = _attention(qvt, 5, HEAD_DIM, kpad3, 2, qvt, 2, causal_bias, q_bias=moba_qb, k_bias=moba_kb)
        y_mla = _attention(lqt, 0, MLA_QK_DIM, kpad3, 3, qvt, 6, causal_bias)
        ys = [y.reshape(m, BRANCH_WIDTH) for y in (y_fox, y_dil, y_moba, y_mla)]
        x2 = _merge(x2, vec(g_pre_mix[l]), wgate_all[l], ys, wb_all[l], wo_all[l], vec(g_post_mix[l]))
        x2 = _mlp(x2, vec(g_pre_mlp[l]), wup_all[l], wdown_all[l], vec(g_post_mlp[l]))
    return x2.reshape(b, s, d)
```

```python
import functools

import numpy as np
import jax
import jax.numpy as jnp
from jax import lax
from jax.experimental import pallas as pl
from jax.experimental.pallas import tpu as pltpu

D_MODEL = 1024
SEQ = 2048
HEAD_DIM = 64
N_HEADS = 4
BRANCH_WIDTH = N_HEADS * HEAD_DIM
HEAD_PAD = 128
QK_PAD = N_HEADS * HEAD_PAD
BIAS_SLOTS = 16
ROT_DIM = HEAD_DIM // 4
ROPE_THETA = 500000.0
DILATED_PAIRS = ((128, 1), (512, 4), (2048, 16))
MOBA_BLOCK = 256
MOBA_TOPK = 3
N_MOBA_BLOCKS = SEQ // MOBA_BLOCK
MLA_Q_RANK = 256
MLA_KV_RANK = 128
MLA_NOPE_DIM = 64
MLA_ROPE_DIM = 32
MLA_QK_DIM = MLA_NOPE_DIM + MLA_ROPE_DIM
D_FF = 4 * D_MODEL
NORM_EPS = 1e-6
NEG = -1e30
LOG2E = 1.4426950408889634

TOKEN_TILE = 512
ATTN_TILE = 256
ACC_ROWS = HEAD_DIM + 16
DIAG_STEPS_PER_TRIP = 8
BELOW_STEPS_PER_TRIP = 14
FF_CHUNK = 512
MERGE_CHUNK = 256
VMEM_LIMIT = 56 * 1024 * 1024

F32 = jnp.float32
BF16 = jnp.bfloat16


def _dot(a, b):
    return jnp.dot(a, b, preferred_element_type=F32)


def _dot_nt(a, b):
    return lax.dot_general(a, b, (((1,), (1,)), ((), ())), preferred_element_type=F32)


def _rms(x, g):
    return x * lax.rsqrt(jnp.mean(x * x, axis=-1, keepdims=True) + NORM_EPS) * g


def _rope(y, a, bm, bp, half, axis):
    n = y.shape[axis]
    return y * a + pltpu.roll(y, n - half, axis) * bm + pltpu.roll(y, half, axis) * bp


def _split3(x):
    hi = x.astype(BF16)
    r = x - hi.astype(F32)
    mid = r.astype(BF16)
    lo = (r - mid.astype(F32)).astype(BF16)
    return hi, mid, lo


def _rope_tables():
    def tables(dim):
        inv_freq = 1.0 / (ROPE_THETA ** (jnp.arange(0, dim, 2, dtype=F32) / dim))
        ang = jnp.arange(SEQ, dtype=F32)[:, None] * inv_freq[None, :]
        return jnp.cos(ang), jnp.sin(ang)

    def expand(cos, sin, width, period, start, half):
        pos = np.arange(width) % period - start
        first = (pos >= 0) & (pos < half)
        second = (pos >= half) & (pos < 2 * half)
        idx = np.where(first, pos, np.where(second, pos - half, 0))
        a = jnp.where((first | second)[None, :], cos[:, idx], 1.0)
        bm = jnp.where(first[None, :], -sin[:, idx], 0.0)
        bp = jnp.where(second[None, :], sin[:, idx], 0.0)
        return a, bm, bp

    cos_p, sin_p = tables(ROT_DIM)
    cos_m, sin_m = tables(MLA_ROPE_DIM)
    partial = expand(cos_p, sin_p, BRANCH_WIDTH, HEAD_DIM, 0, ROT_DIM // 2)
    mla_k = expand(cos_m, sin_m, HEAD_PAD, HEAD_PAD, MLA_NOPE_DIM, MLA_ROPE_DIM // 2)
    mla_q = expand(cos_m, sin_m, MLA_QK_DIM, MLA_QK_DIM, MLA_NOPE_DIM, MLA_ROPE_DIM // 2)
    return partial, mla_k, mla_q


def _fox_constants():
    to_k = np.zeros((HEAD_PAD, HEAD_PAD), np.float32)
    to_q = np.zeros((N_HEADS * BIAS_SLOTS, HEAD_PAD), np.float32)
    q_ones = np.zeros((N_HEADS * BIAS_SLOTS, 1), np.float32)
    k_ones = np.zeros((1, HEAD_PAD), np.float32)
    for h in range(N_HEADS):
        for j in range(3):
            to_k[4 * j + h, HEAD_DIM + BIAS_SLOTS * h + 3 + j] = 1.0
            to_q[BIAS_SLOTS * h + j, 4 * j + h] = 1.0
        q_ones[BIAS_SLOTS * h + 3:BIAS_SLOTS * h + 6] = 1.0
        k_ones[0, HEAD_DIM + BIAS_SLOTS * h:HEAD_DIM + BIAS_SLOTS * h + 3] = 1.0
    return [jnp.asarray(to_k, BF16), jnp.asarray(to_q, BF16), jnp.asarray(q_ones), jnp.asarray(k_ones)]


def _moba_key_bias():
    out = np.zeros((1, SEQ, HEAD_PAD), np.float32)
    blk = np.arange(SEQ) // MOBA_BLOCK
    for h in range(N_HEADS):
        for n in range(N_MOBA_BLOCKS):
            out[0, :, HEAD_DIM + BIAS_SLOTS * h + n] = blk == n
    return jnp.asarray(out, BF16)


def _block_indicator():
    ind = np.zeros((HEAD_PAD, SEQ), np.float32)
    for h in range(N_HEADS):
        for n in range(N_MOBA_BLOCKS):
            ind[N_MOBA_BLOCKS * h + n, n * MOBA_BLOCK:(n + 1) * MOBA_BLOCK] = 1.0
    return jnp.asarray(ind, BF16)


def _block_head_mask():
    m = np.zeros((HEAD_PAD, QK_PAD), np.float32)
    for h in range(N_HEADS):
        m[N_MOBA_BLOCKS * h:N_MOBA_BLOCKS * (h + 1), h * HEAD_PAD:h * HEAD_PAD + HEAD_DIM] = 1.0
    return jnp.asarray(m)


def _causal_bias():
    key = np.arange(ATTN_TILE)[:, None]
    query = np.arange(ATTN_TILE)[None, :]
    return jnp.asarray(np.where(key <= query, 0.0, NEG).astype(np.float32)[None])


def _dilated_bias():
    r = np.arange(ATTN_TILE)[None, :]
    c = np.arange(ATTN_TILE)[:, None]
    count = np.zeros((SEQ // ATTN_TILE, ATTN_TILE, ATTN_TILE), np.float32)
    for delta in range(SEQ // ATTN_TILE):
        d = delta * ATTN_TILE + r - c
        for window, dil in DILATED_PAIRS:
            count[delta] += (d >= 0) & (d <= window) & (d % dil == 0)
    return jnp.asarray(np.where(count > 0, np.log2(np.maximum(count, 1.0)), NEG).astype(np.float32))


def _full(shape):
    return pl.BlockSpec(shape, lambda *_: (0,) * len(shape), pipeline_mode=pl.Buffered(1))


def _params(*sem):
    return pltpu.CompilerParams(dimension_semantics=sem, vmem_limit_bytes=VMEM_LIMIT)


def _inproj_kernel(x_ref, g_ref, wk_ref, wt_ref, wsmall_ref, wf_ref, gq_ref, gkv_ref, wuqt_ref, wuk_ref,
                   wuvt_ref, ra_ref, rbm_ref, rbp_ref, rat_ref, rbmt_ref, rbpt_ref, mra_ref, mrbm_ref,
                   mrbp_ref, mrat_ref, mrbmt_ref, mrbpt_ref, kpad_ref, qvt_ref, lqt_ref, f_ref):
    hf = _rms(x_ref[...], g_ref[...])
    h = hf.astype(BF16)
    tm = h.shape[0]

    lane = lax.broadcasted_iota(jnp.int32, f_ref.shape, 1)
    f = jnp.zeros(f_ref.shape, F32)
    for j in range(N_HEADS):
        col = jnp.sum(hf * wf_ref[j:j + 1, :], axis=-1, keepdims=True)
        f = jnp.where(lane == j, col, f)
    f_ref[...] = f

    small = _dot(h, wsmall_ref[...])
    half = MLA_ROPE_DIM // 2
    cq = _rms(small[:, 0:MLA_Q_RANK], gq_ref[...]).astype(BF16)
    ckv = _rms(small[:, MLA_Q_RANK:MLA_Q_RANK + MLA_KV_RANK], gkv_ref[...]).astype(BF16)
    kr = _rope(small[:, MLA_Q_RANK + MLA_KV_RANK:], mra_ref[...], mrbm_ref[...], mrbp_ref[...], half, 1)
    qf_t = _dot_nt(wuqt_ref[...], cq)
    kf = _dot(ckv, wuk_ref[...])
    for hd in range(N_HEADS):
        rows = slice(hd * MLA_QK_DIM, (hd + 1) * MLA_QK_DIM)
        q_h = _rope(qf_t[rows, :], mrat_ref[...], mrbmt_ref[...], mrbpt_ref[...], half, 0)
        lqt_ref[0, rows, :] = (q_h * (MLA_QK_DIM ** -0.5 * LOG2E)).astype(BF16)
        col = 3 * QK_PAD + hd * HEAD_PAD
        kpad_ref[:, col:col + HEAD_PAD] = (kf[:, hd * HEAD_PAD:(hd + 1) * HEAD_PAD] + kr).astype(BF16)
    qvt_ref[0, 6 * BRANCH_WIDTH:, :] = _dot_nt(wuvt_ref[...], ckv).astype(BF16)

    low_half = lax.broadcasted_iota(jnp.int32, (tm, HEAD_PAD), 1) < HEAD_DIM
    for c in range(3):
        y = _dot(h, wk_ref[:, c * BRANCH_WIDTH:(c + 1) * BRANCH_WIDTH])
        if c >= 1:
            y = _rope(y, ra_ref[...], rbm_ref[...], rbp_ref[...], ROT_DIM // 2, 1)
        for pair in range(2):
            both = y[:, pair * HEAD_PAD:(pair + 1) * HEAD_PAD]
            for odd, src in enumerate((both, pltpu.roll(both, HEAD_DIM, 1))):
                col = c * QK_PAD + (2 * pair + odd) * HEAD_PAD
                kpad_ref[:, col:col + HEAD_PAD] = jnp.where(low_half, src, 0.0).astype(BF16)

    t_all = _dot_nt(wt_ref[...], h)
    for c in range(6):
        rows = slice(c * BRANCH_WIDTH, (c + 1) * BRANCH_WIDTH)
        y = t_all[rows, :]
        if c >= 4:
            y = _rope(y, rat_ref[...], rbmt_ref[...], rbpt_ref[...], ROT_DIM // 2, 0)
        if c >= 3:
            y = y * (HEAD_DIM ** -0.5 * LOG2E)
        qvt_ref[0, rows, :] = y.astype(BF16)


def _inproj(x2, g, weights, rope_p, rope_mk, rope_mq):
    m = x2.shape[0]
    tm = TOKEN_TILE
    seq_tiles = SEQ // tm
    row = lambda i: (i, 0)
    by_batch = lambda i: (i // seq_tiles, 0, i % seq_tiles)
    seq_rows = lambda width: pl.BlockSpec((tm, width), lambda i: (i % seq_tiles, 0))
    seq_cols = lambda height: pl.BlockSpec((height, tm), lambda i: (0, i % seq_tiles))
    tables = ([seq_rows(BRANCH_WIDTH)] * 3 + [seq_cols(BRANCH_WIDTH)] * 3
              + [seq_rows(HEAD_PAD)] * 3 + [seq_cols(MLA_QK_DIM)] * 3)
    return pl.pallas_call(
        _inproj_kernel,
        grid=(m // tm,),
        in_specs=[pl.BlockSpec((tm, D_MODEL), row), _full((1, D_MODEL))] + [_full(w.shape) for w in weights]
        + tables,
        out_specs=[pl.BlockSpec((tm, 4 * QK_PAD), row), pl.BlockSpec((1, 7 * BRANCH_WIDTH, tm), by_batch),
                   pl.BlockSpec((1, N_HEADS * MLA_QK_DIM, tm), by_batch), pl.BlockSpec((tm, HEAD_PAD), row)],
        out_shape=[jax.ShapeDtypeStruct((m, 4 * QK_PAD), BF16),
                   jax.ShapeDtypeStruct((m // SEQ, 7 * BRANCH_WIDTH, SEQ), BF16),
                   jax.ShapeDtypeStruct((m // SEQ, N_HEADS * MLA_QK_DIM, SEQ), BF16),
                   jax.ShapeDtypeStruct((m, HEAD_PAD), F32)],
        compiler_params=_params("parallel"),
        name="inproj",
    )(x2, g, *weights, *rope_p, *[t.T for t in rope_p], *rope_mk, *[t.T for t in rope_mq])


def _fox_bias_kernel(f_ref, bf_ref, to_k_ref, to_q_ref, q_ones_ref, k_ones_ref, qb_ref, kb_ref):
    chunk = 256
    ri = lax.broadcasted_iota(jnp.int32, (chunk, chunk), 0)
    ci = lax.broadcasted_iota(jnp.int32, (chunk, chunk), 1)
    tril = jnp.where(ri >= ci, 1.0, 0.0).astype(BF16)
    lane = lax.broadcasted_iota(jnp.int32, (chunk, HEAD_PAD), 1)
    carry = jnp.zeros((1, HEAD_PAD), F32)
    for c in range(SEQ // chunk):
        rows = slice(c * chunk, (c + 1) * chunk)
        z = f_ref[0, rows, :] + bf_ref[...]
        logf = jnp.minimum(z, 0.0) - jnp.log(1.0 + jnp.exp(-jnp.abs(z)))
        hi, mid, lo = _split3(logf)
        cum = _dot(tril, hi) + _dot(tril, mid) + _dot(tril, lo) + carry
        carry = cum[chunk - 1:chunk, :]
        c_hi, c_mid, c_lo = (p.astype(F32) for p in _split3(cum * LOG2E))
        packed = jnp.where(lane < N_HEADS, c_hi,
                           jnp.where(lane < 2 * N_HEADS, pltpu.roll(c_mid, N_HEADS, 1),
                                     pltpu.roll(c_lo, 2 * N_HEADS, 1))).astype(BF16)
        kb_ref[0, rows, :] = (k_ones_ref[...] - _dot(packed, to_k_ref[...])).astype(BF16)
        qb_ref[0, :, rows] = (_dot_nt(to_q_ref[...], packed) + q_ones_ref[...]).astype(BF16)


def _fox_bias(fpre3, bf, consts):
    b = fpre3.shape[0]
    return pl.pallas_call(
        _fox_bias_kernel,
        grid=(b,),
        in_specs=[pl.BlockSpec((1, SEQ, HEAD_PAD), lambda i: (i, 0, 0)), _full(bf.shape)]
        + [_full(c.shape) for c in consts],
        out_specs=[pl.BlockSpec((1, N_HEADS * BIAS_SLOTS, SEQ), lambda i: (i, 0, 0)),
                   pl.BlockSpec((1, SEQ, HEAD_PAD), lambda i: (i, 0, 0))],
        out_shape=[jax.ShapeDtypeStruct((b, N_HEADS * BIAS_SLOTS, SEQ), BF16),
                   jax.ShapeDtypeStruct((b, SEQ, HEAD_PAD), BF16)],
        compiler_params=_params("parallel"),
        name="fox_bias",
    )(fpre3, bf, *consts)


def _moba_bias_kernel(qt_ref, k_ref, ind_ref, hmask_ref, qb_ref):
    kmean = _dot(ind_ref[...], k_ref[0]) * (1.0 / MOBA_BLOCK) * hmask_ref[...]
    pieces = _split3(kmean)
    zeros = jnp.zeros((HEAD_PAD - HEAD_DIM, SEQ), BF16)
    qt_pad = jnp.concatenate(
        [part for h in range(N_HEADS) for part in (qt_ref[0, h * HEAD_DIM:(h + 1) * HEAD_DIM, :], zeros)], axis=0)
    gate = _dot(pieces[0], qt_pad) + _dot(pieces[1], qt_pad) + _dot(pieces[2], qt_pad)
    blk = lax.broadcasted_iota(jnp.int32, (N_MOBA_BLOCKS, SEQ), 0)
    own = lax.broadcasted_iota(jnp.int32, (N_MOBA_BLOCKS, SEQ), 1) // MOBA_BLOCK
    for h in range(N_HEADS):
        g = gate[h * N_MOBA_BLOCKS:(h + 1) * N_MOBA_BLOCKS, :]
        rank = jnp.zeros(g.shape, F32)
        for r in range(1, N_MOBA_BLOCKS):
            other = pltpu.roll(g, N_MOBA_BLOCKS - r, 0)
            other_blk = jnp.where(blk + r >= N_MOBA_BLOCKS, blk + r - N_MOBA_BLOCKS, blk + r)
            ahead = (other > g) | ((other == g) & (other_blk < blk))
            rank = rank + jnp.where(ahead & (other_blk < own), 1.0, 0.0)
        keep = ((blk < own) & (rank < MOBA_TOPK)) | (blk == own)
        bias = jnp.concatenate([jnp.where(keep, 0.0, NEG), jnp.zeros(g.shape, F32)], axis=0)
        qb_ref[0, h * BIAS_SLOTS:(h + 1) * BIAS_SLOTS, :] = bias.astype(BF16)


def _moba_bias(qvt, q_row, kpad3, k_col, ind, hmask):
    b = qvt.shape[0]
    return pl.pallas_call(
        _moba_bias_kernel,
        grid=(b,),
        in_specs=[pl.BlockSpec((1, BRANCH_WIDTH, SEQ), lambda i: (i, q_row, 0)),
                  pl.BlockSpec((1, SEQ, QK_PAD), lambda i: (i, 0, k_col)),
                  _full(ind.shape), _full(hmask.shape)],
        out_specs=pl.BlockSpec((1, N_HEADS * BIAS_SLOTS, SEQ), lambda i: (i, 0, 0)),
        out_shape=jax.ShapeDtypeStruct((b, N_HEADS * BIAS_SLOTS, SEQ), BF16),
        compiler_params=_params("parallel"),
        name="moba_bias",
    )(qvt, kpad3, ind, hmask)


def _attn_kernel(*refs, q_dim, biased, dilated):
    if biased:
        qt_ref, qb_ref, k_ref, kb_ref, vt_ref, tab_ref, o_ref, sa_ref, sb_ref, m_ref, acc_ref = refs
    else:
        qt_ref, k_ref, vt_ref, tab_ref, o_ref, sa_ref, sb_ref, m_ref, acc_ref = refs
    t = ATTN_TILE
    n_tiles = SEQ // t
    ones_rows = jnp.ones((ACC_ROWS - HEAD_DIM, t), BF16)

    m_ref[...] = jnp.full(m_ref.shape, NEG, F32)
    acc_ref[...] = jnp.zeros(acc_ref.shape, F32)

    def scores(i, j, s_ref):
        qo = pl.multiple_of(i * t, t)
        ko = pl.multiple_of(j * t, t)
        for h in range(N_HEADS):
            parts = [qt_ref[0, h * q_dim:(h + 1) * q_dim, pl.ds(qo, t)]]
            k_op = k_ref[0, pl.ds(ko, t), h * HEAD_PAD:(h + 1) * HEAD_PAD]
            if biased:
                k_op = k_op + kb_ref[0, pl.ds(ko, t), :]
                if h > 0:
                    parts.append(jnp.zeros((h * BIAS_SLOTS, t), BF16))
                parts.append(qb_ref[0, h * BIAS_SLOTS:(h + 1) * BIAS_SLOTS, pl.ds(qo, t)])
            fill = HEAD_PAD - sum(p.shape[0] for p in parts)
            if fill:
                parts.append(jnp.zeros((fill, t), BF16))
            s_ref[h] = _dot(k_op, jnp.concatenate(parts, axis=0))

    def update(i, j, s_ref, bias):
        qo = pl.multiple_of(i * t, t)
        ko = pl.multiple_of(j * t, t)
        for h in range(N_HEADS):
            s = s_ref[h] if bias is None else s_ref[h] + bias
            m_old = m_ref[h, :, pl.ds(qo, t)]
            m_new = jnp.maximum(m_old, jnp.max(s, axis=0, keepdims=True))
            m_ref[h, :, pl.ds(qo, t)] = m_new
            p = jnp.exp2(s - m_new).astype(BF16)
            v_aug = jnp.concatenate([vt_ref[0, h * HEAD_DIM:(h + 1) * HEAD_DIM, pl.ds(ko, t)], ones_rows], axis=0)
            rows = slice(h * ACC_ROWS, (h + 1) * ACC_ROWS)
            acc_ref[rows, pl.ds(qo, t)] = (acc_ref[rows, pl.ds(qo, t)] * jnp.exp2(m_old - m_new)
                                           + _dot(v_aug, p))

    def pipeline(first, succ, n_steps, per_trip, bias_of):
        def trip(_, ij):
            cur = ij
            for step in range(per_trip):
                s_cur, s_next = (sa_ref, sb_ref) if step % 2 == 0 else (sb_ref, sa_ref)
                nxt = succ(*cur)
                scores(*nxt, s_next)
                update(*cur, s_cur, bias_of(*cur))
                cur = nxt
            return cur

        assert n_steps % per_trip == 0 and per_trip % 2 == 0
        scores(*first, sa_ref)
        lax.fori_loop(0, n_steps // per_trip, trip, (jnp.int32(first[0]), jnp.int32(first[1])))

    def succ_diag(i, j):
        nxt = jnp.minimum(i + 1, n_tiles - 1)
        return nxt, nxt

    def succ_below(i, j):
        last = j == i - 1
        return jnp.where(last, jnp.minimum(i + 1, n_tiles - 1), i), jnp.where(last, 0, j + 1)

    pipeline((0, 0), succ_diag, n_tiles, DIAG_STEPS_PER_TRIP, lambda i, j: tab_ref[0])
    pipeline((1, 0), succ_below, n_tiles * (n_tiles - 1) // 2, BELOW_STEPS_PER_TRIP,
             (lambda i, j: tab_ref[i - j]) if dilated else (lambda i, j: None))

    for i in range(n_tiles):
        q_cols = slice(i * t, (i + 1) * t)
        heads = []
        for h in range(N_HEADS):
            denom = acc_ref[h * ACC_ROWS + HEAD_DIM:h * ACC_ROWS + HEAD_DIM + 1, q_cols]
            heads.append(acc_ref[h * ACC_ROWS:h * ACC_ROWS + HEAD_DIM, q_cols] * (1.0 / denom))
        o_ref[0, q_cols, :] = jnp.concatenate(heads, axis=0).T.astype(o_ref.dtype)


def _attention(qt, q_row, q_dim, k, k_col, vt, v_row, table, dilated=False, q_bias=None, k_bias=None):
    b = qt.shape[0]
    t = ATTN_TILE
    biased = q_bias is not None
    in_specs = [pl.BlockSpec((1, N_HEADS * q_dim, SEQ), lambda bi: (bi, q_row, 0))]
    args = [qt]
    if biased:
        in_specs.append(pl.BlockSpec((1, N_HEADS * BIAS_SLOTS, SEQ), lambda bi: (bi, 0, 0)))
        args.append(q_bias)
    in_specs.append(pl.BlockSpec((1, SEQ, QK_PAD), lambda bi: (bi, 0, k_col)))
    args.append(k)
    if biased:
        per_batch = k_bias.shape[0] > 1
        in_specs.append(pl.BlockSpec((1, SEQ, HEAD_PAD), lambda bi: (bi if per_batch else 0, 0, 0)))
        args.append(k_bias)
    in_specs += [pl.BlockSpec((1, BRANCH_WIDTH, SEQ), lambda bi: (bi, v_row, 0)), _full(table.shape)]
    args += [vt, table]
    return pl.pallas_call(
        functools.partial(_attn_kernel, q_dim=q_dim, biased=biased, dilated=dilated),
        grid=(b,),
        in_specs=in_specs,
        out_specs=pl.BlockSpec((1, SEQ, BRANCH_WIDTH), lambda bi: (bi, 0, 0)),
        out_shape=jax.ShapeDtypeStruct((b, SEQ, BRANCH_WIDTH), BF16),
        scratch_shapes=[pltpu.VMEM((N_HEADS, t, t), F32), pltpu.VMEM((N_HEADS, t, t), F32),
                        pltpu.VMEM((N_HEADS, 1, SEQ), F32), pltpu.VMEM((N_HEADS * ACC_ROWS, SEQ), F32)],
        compiler_params=_params("parallel"),
        name="attn_dil" if dilated else ("attn_bias" if biased else "attn"),
    )(*args)


def _merge_kernel(x_ref, gpre_ref, wg_ref, y0_ref, y1_ref, y2_ref, y3_ref, wb_ref, wo_ref, gpost_ref,
                  o_ref):
    x = x_ref[...]
    h = _rms(x, gpre_ref[...]).astype(BF16)
    mix = None
    for c in range(D_MODEL // MERGE_CHUNK):
        cols = slice(c * MERGE_CHUNK, (c + 1) * MERGE_CHUNK)
        merged = None
        for n, y_ref in enumerate((y0_ref, y1_ref, y2_ref, y3_ref)):
            gate = jax.nn.sigmoid(_dot(h, wg_ref[:, n * D_MODEL + c * MERGE_CHUNK:
                                                 n * D_MODEL + (c + 1) * MERGE_CHUNK]))
            term = gate * _dot(y_ref[...], wb_ref[n, :, cols])
            merged = term if merged is None else merged + term
        part = _dot(merged.astype(BF16), wo_ref[cols, :])
        mix = part if mix is None else mix + part
    o_ref[...] = x + _rms(mix, gpost_ref[...])


def _merge(x2, gpre, wg, ys, wb, wo, gpost):
    m = x2.shape[0]
    tm = TOKEN_TILE
    row = lambda i: (i, 0)
    yspec = pl.BlockSpec((tm, BRANCH_WIDTH), row)
    return pl.pallas_call(
        _merge_kernel,
        grid=(m // tm,),
        in_specs=[pl.BlockSpec((tm, D_MODEL), row), _full(gpre.shape), _full(wg.shape),
                  yspec, yspec, yspec, yspec, _full(wb.shape), _full(wo.shape), _full(gpost.shape)],
        out_specs=pl.BlockSpec((tm, D_MODEL), row),
        out_shape=jax.ShapeDtypeStruct((m, D_MODEL), F32),
        compiler_params=_params("parallel"),
        name="merge",
    )(x2, gpre, wg, *ys, wb, wo, gpost)


def _mlp_kernel(x_ref, gpre_ref, wup_ref, wdown_ref, gpost_ref, o_ref):
    x = x_ref[...]
    h = _rms(x, gpre_ref[...]).astype(BF16)
    acc = None
    for c in range(D_FF // FF_CHUNK):
        cols = slice(c * FF_CHUNK, (c + 1) * FF_CHUNK)
        up = jnp.maximum(_dot(h, wup_ref[:, cols]), 0.0)
        term = _dot((up * up).astype(BF16), wdown_ref[cols, :])
        acc = term if acc is None else acc + term
    o_ref[...] = x + _rms(acc, gpost_ref[...])


def _mlp(x2, gpre, wup, wdown, gpost):
    m = x2.shape[0]
    tm = TOKEN_TILE
    row = lambda i: (i, 0)
    return pl.pallas_call(
        _mlp_kernel,
        grid=(m // tm,),
        in_specs=[pl.BlockSpec((tm, D_MODEL), row), _full(gpre.shape), _full(wup.shape),
                  _full(wdown.shape), _full(gpost.shape)],
        out_specs=pl.BlockSpec((tm, D_MODEL), row),
        out_shape=jax.ShapeDtypeStruct((m, D_MODEL), F32),
        compiler_params=_params("parallel"),
        name="mlp",
    )(x2, gpre, wup, wdown, gpost)


def kernel(x, w_in, b_forget, g_cq, g_ckv, w_uq, w_uk, w_uv, w_branch, w_out, w_up, w_down,
           g_pre_mix, g_post_mix, g_pre_mlp, g_post_mlp):
    b, s, d = x.shape
    assert (s, d) == (SEQ, D_MODEL)
    depth = w_in.shape[0]
    m = b * s
    rope_p, rope_mk, rope_mq = _rope_tables()
    fox_consts = _fox_constants()
    moba_ind, moba_hmask, moba_kb = _block_indicator(), _block_head_mask(), _moba_key_bias()
    dil_bias = _dilated_bias()
    causal_bias = _causal_bias()

    bw = BRANCH_WIDTH
    o_f = 3 * bw
    o_dil = o_f + N_HEADS
    o_moba = o_dil + 3 * bw
    o_cq = o_moba + 3 * bw
    o_ckv = o_cq + MLA_Q_RANK
    o_kr = o_ckv + MLA_KV_RANK
    o_gate = o_kr + MLA_ROPE_DIM

    qkv_at = (0, o_dil, o_moba)
    wk_all = jnp.concatenate([w_in[:, :, o + bw:o + 2 * bw] for o in qkv_at], axis=2).astype(BF16)
    wt_all = jnp.swapaxes(jnp.concatenate([w_in[:, :, o + 2 * bw:o + 3 * bw] for o in qkv_at]
                                          + [w_in[:, :, o:o + bw] for o in qkv_at], axis=2), 1, 2).astype(BF16)
    wsmall_all = jnp.concatenate(
        [w_in[:, :, o_cq:o_kr], jnp.zeros((depth, d, MLA_NOPE_DIM), F32), w_in[:, :, o_kr:o_gate],
         jnp.zeros((depth, d, HEAD_PAD - MLA_QK_DIM), F32)], axis=2).astype(BF16)
    wf_all = jnp.concatenate([jnp.swapaxes(w_in[:, :, o_f:o_dil], 1, 2),
                              jnp.zeros((depth, 8 - N_HEADS, d), F32)], axis=1)
    wgate_all = w_in[:, :, o_gate:].astype(BF16)
    bf_all = jnp.concatenate([b_forget, jnp.zeros((depth, HEAD_PAD - N_HEADS), F32)], axis=1)
    wuqt_all = jnp.swapaxes(w_uq, 1, 2).astype(BF16)
    wuk_all = jnp.concatenate(
        [part for h in range(N_HEADS) for part in (w_uk[:, :, h * MLA_NOPE_DIM:(h + 1) * MLA_NOPE_DIM],
                                                   jnp.zeros((depth, MLA_KV_RANK, HEAD_PAD - MLA_NOPE_DIM), F32))],
        axis=2).astype(BF16)
    wuvt_all = jnp.swapaxes(w_uv, 1, 2).astype(BF16)
    wb_all, wo_all = w_branch.astype(BF16), w_out.astype(BF16)
    wup_all, wdown_all = w_up.astype(BF16), w_down.astype(BF16)

    x2 = x.reshape(m, d)
    vec = lambda g: g.reshape(1, -1)
    for l in range(depth):
        weights = [wk_all[l], wt_all[l], wsmall_all[l], wf_all[l], vec(g_cq[l]), vec(g_ckv[l]),
                   wuqt_all[l], wuk_all[l], wuvt_all[l]]
        kpad, qvt, lqt, fpre = _inproj(x2, vec(g_pre_mix[l]), weights, rope_p, rope_mk, rope_mq)
        kpad3 = kpad.reshape(b, s, 4 * QK_PAD)
        fox_qb, fox_kb = _fox_bias(fpre.reshape(b, s, HEAD_PAD), bf_all[l:l + 1], fox_consts)
        moba_qb = _moba_bias(qvt, 5, kpad3, 2, moba_ind, moba_hmask)
        y_fox = _attention(qvt, 3, HEAD_DIM, kpad3, 0, qvt, 0, causal_bias, q_bias=fox_qb, k_bias=fox_kb)
        y_dil = _attention(qvt, 4, HEAD_DIM, kpad3, 1, qvt, 1, dil_bias, dilated=True)
        y_moba = _attention(qvt, 5, HEAD_DIM, kpad3, 2, qvt, 2, causal_bias, q_bias=moba_qb, k_bias=moba_kb)
        y_mla = _attention(lqt, 0, MLA_QK_DIM, kpad3, 3, qvt, 6, causal_bias)
        ys = [y.reshape(m, BRANCH_WIDTH) for y in (y_fox, y_dil, y_moba, y_mla)]
        x2 = _merge(x2, vec(g_pre_mix[l]), wgate_all[l], ys, wb_all[l], wo_all[l], vec(g_post_mix[l]))
        x2 = _mlp(x2, vec(g_pre_mlp[l]), wup_all[l], wdown_all[l], vec(g_post_mlp[l]))
    return x2.reshape(b, s, d)
```

```python
import functools

import numpy as np
import jax
import jax.numpy as jnp
from jax import lax
from jax.experimental import pallas as pl
from jax.experimental.pallas import tpu as pltpu

D_MODEL = 1024
SEQ = 2048
HEAD_DIM = 64
N_HEADS = 4
BRANCH_WIDTH = N_HEADS * HEAD_DIM
HEAD_PAD = 128
QK_PAD = N_HEADS * HEAD_PAD
BIAS_SLOTS = 16
ROT_DIM = HEAD_DIM // 4
ROPE_THETA = 500000.0
DILATED_PAIRS = ((128, 1), (512, 4), (2048, 16))
MOBA_BLOCK = 256
MOBA_TOPK = 3
N_MOBA_BLOCKS = SEQ // MOBA_BLOCK
MLA_Q_RANK = 256
MLA_KV_RANK = 128
MLA_NOPE_DIM = 64
MLA_ROPE_DIM = 32
MLA_QK_DIM = MLA_NOPE_DIM + MLA_ROPE_DIM
D_FF = 4 * D_MODEL
NORM_EPS = 1e-6
NEG = -1e30
LOG2E = 1.4426950408889634

TOKEN_TILE = 512
ATTN_TILE = 256
ACC_ROWS = HEAD_DIM + 16
DIAG_STEPS_PER_TRIP = 8
BELOW_STEPS_PER_TRIP = 14
FF_CHUNK = 512
MERGE_CHUNK = 256
VMEM_LIMIT = 56 * 1024 * 1024

F32 = jnp.float32
BF16 = jnp.bfloat16


def _dot(a, b):
    return jnp.dot(a, b, preferred_element_type=F32)


def _dot_nt(a, b):
    return lax.dot_general(a, b, (((1,), (1,)), ((), ())), preferred_element_type=F32)


def _rms(x, g):
    return x * lax.rsqrt(jnp.mean(x * x, axis=-1, keepdims=True) + NORM_EPS) * g


def _rope(y, a, bm, bp, half, axis):
    n = y.shape[axis]
    return y * a + pltpu.roll(y, n - half, axis) * bm + pltpu.roll(y, half, axis) * bp


def _split3(x):
    hi = x.astype(BF16)
    r = x - hi.astype(F32)
    mid = r.astype(BF16)
    lo = (r - mid.astype(F32)).astype(BF16)
    return hi, mid, lo


def _rope_tables():
    def tables(dim):
        inv_freq = 1.0 / (ROPE_THETA ** (jnp.arange(0, dim, 2, dtype=F32) / dim))
        ang = jnp.arange(SEQ, dtype=F32)[:, None] * inv_freq[None, :]
        return jnp.cos(ang), jnp.sin(ang)

    def expand(cos, sin, width, period, start, half):
        pos = np.arange(width) % period - start
        first = (pos >= 0) & (pos < half)
        second = (pos >= half) & (pos < 2 * half)
        idx = np.where(first, pos, np.where(second, pos - half, 0))
        a = jnp.where((first | second)[None, :], cos[:, idx], 1.0)
        bm = jnp.where(first[None, :], -sin[:, idx], 0.0)
        bp = jnp.where(second[None, :], sin[:, idx], 0.0)
        return a, bm, bp

    cos_p, sin_p = tables(ROT_DIM)
    cos_m, sin_m = tables(MLA_ROPE_DIM)
    partial = expand(cos_p, sin_p, BRANCH_WIDTH, HEAD_DIM, 0, ROT_DIM // 2)
    mla_k = expand(cos_m, sin_m, HEAD_PAD, HEAD_PAD, MLA_NOPE_DIM, MLA_ROPE_DIM // 2)
    mla_q = expand(cos_m, sin_m, MLA_QK_DIM, MLA_QK_DIM, MLA_NOPE_DIM, MLA_ROPE_DIM // 2)
    return partial, mla_k, mla_q


def _fox_constants():
    to_k = np.zeros((HEAD_PAD, HEAD_PAD), np.float32)
    to_q = np.zeros((N_HEADS * BIAS_SLOTS, HEAD_PAD), np.float32)
    q_ones = np.zeros((N_HEADS * BIAS_SLOTS, 1), np.float32)
    k_ones = np.zeros((1, HEAD_PAD), np.float32)
    for h in range(N_HEADS):
        for j in range(3):
            to_k[4 * j + h, HEAD_DIM + BIAS_SLOTS * h + 3 + j] = 1.0
            to_q[BIAS_SLOTS * h + j, 4 * j + h] = 1.0
        q_ones[BIAS_SLOTS * h + 3:BIAS_SLOTS * h + 6] = 1.0
        k_ones[0, HEAD_DIM + BIAS_SLOTS * h:HEAD_DIM + BIAS_SLOTS * h + 3] = 1.0
    return [jnp.asarray(to_k, BF16), jnp.asarray(to_q, BF16), jnp.asarray(q_ones), jnp.asarray(k_ones)]


def _moba_key_bias():
    out = np.zeros((1, SEQ, HEAD_PAD), np.float32)
    blk = np.arange(SEQ) // MOBA_BLOCK
    for h in range(N_HEADS):
        for n in range(N_MOBA_BLOCKS):
            out[0, :, HEAD_DIM + BIAS_SLOTS * h + n] = blk == n
    return jnp.asarray(out, BF16)


def _block_indicator():
    ind = np.zeros((HEAD_PAD, SEQ), np.float32)
    for h in range(N_HEADS):
        for n in range(N_MOBA_BLOCKS):
            ind[N_MOBA_BLOCKS * h + n, n * MOBA_BLOCK:(n + 1) * MOBA_BLOCK] = 1.0
    return jnp.asarray(ind, BF16)


def _block_head_mask():
    m = np.zeros((HEAD_PAD, QK_PAD), np.float32)
    for h in range(N_HEADS):
        m[N_MOBA_BLOCKS * h:N_MOBA_BLOCKS * (h + 1), h * HEAD_PAD:h * HEAD_PAD + HEAD_DIM] = 1.0
    return jnp.asarray(m)


def _causal_bias():
    key = np.arange(ATTN_TILE)[:, None]
    query = np.arange(ATTN_TILE)[None, :]
    return jnp.asarray(np.where(key <= query, 0.0, NEG).astype(np.float32)[None])


def _dilated_bias():
    r = np.arange(ATTN_TILE)[None, :]
    c = np.arange(ATTN_TILE)[:, None]
    count = np.zeros((SEQ // ATTN_TILE, ATTN_TILE, ATTN_TILE), np.float32)
    for delta in range(SEQ // ATTN_TILE):
        d = delta * ATTN_TILE + r - c
        for window, dil in DILATED_PAIRS:
            count[delta] += (d >= 0) & (d <= window) & (d % dil == 0)
    return jnp.asarray(np.where(count > 0, np.log2(np.maximum(count, 1.0)), NEG).astype(np.float32))


def _full(shape):
    return pl.BlockSpec(shape, lambda *_: (0,) * len(shape), pipeline_mode=pl.Buffered(1))


def _whole(operands):
    arrays, specs = [], []
    for op in operands:
        if isinstance(op, tuple):
            stacked, layer = op
            index = (layer,) + (0,) * (stacked.ndim - 1)
            arrays.append(stacked)
            specs.append(pl.BlockSpec((None,) + stacked.shape[1:], lambda *_, index=index: index,
                                      pipeline_mode=pl.Buffered(1)))
        else:
            arrays.append(op)
            specs.append(_full(op.shape))
    return arrays, specs


def _params(*sem):
    return pltpu.CompilerParams(dimension_semantics=sem, vmem_limit_bytes=VMEM_LIMIT)


def _inproj_kernel(x_ref, g_ref, wk_ref, wt_ref, wsmall_ref, wf_ref, gq_ref, gkv_ref, wuqt_ref, wuk_ref,
                   wuvt_ref, ra_ref, rbm_ref, rbp_ref, rat_ref, rbmt_ref, rbpt_ref, mra_ref, mrbm_ref,
                   mrbp_ref, mrat_ref, mrbmt_ref, mrbpt_ref, kpad_ref, qvt_ref, lqt_ref, f_ref):
    hf = _rms(x_ref[...], g_ref[...])
    h = hf.astype(BF16)
    tm = h.shape[0]

    lane = lax.broadcasted_iota(jnp.int32, f_ref.shape, 1)
    f = jnp.zeros(f_ref.shape, F32)
    for j in range(N_HEADS):
        col = jnp.sum(hf * wf_ref[j:j + 1, :], axis=-1, keepdims=True)
        f = jnp.where(lane == j, col, f)
    f_ref[...] = f

    small = _dot(h, wsmall_ref[...])
    half = MLA_ROPE_DIM // 2
    cq = _rms(small[:, 0:MLA_Q_RANK], gq_ref[...]).astype(BF16)
    ckv = _rms(small[:, MLA_Q_RANK:MLA_Q_RANK + MLA_KV_RANK], gkv_ref[...]).astype(BF16)
    kr = _rope(small[:, MLA_Q_RANK + MLA_KV_RANK:], mra_ref[...], mrbm_ref[...], mrbp_ref[...], half, 1)
    qf_t = _dot_nt(wuqt_ref[...], cq)
    kf = _dot(ckv, wuk_ref[...])
    for hd in range(N_HEADS):
        rows = slice(hd * MLA_QK_DIM, (hd + 1) * MLA_QK_DIM)
        q_h = _rope(qf_t[rows, :], mrat_ref[...], mrbmt_ref[...], mrbpt_ref[...], half, 0)
        lqt_ref[0, rows, :] = (q_h * (MLA_QK_DIM ** -0.5 * LOG2E)).astype(BF16)
        col = 3 * QK_PAD + hd * HEAD_PAD
        kpad_ref[:, col:col + HEAD_PAD] = (kf[:, hd * HEAD_PAD:(hd + 1) * HEAD_PAD] + kr).astype(BF16)
    qvt_ref[0, 6 * BRANCH_WIDTH:, :] = _dot_nt(wuvt_ref[...], ckv).astype(BF16)

    low_half = lax.broadcasted_iota(jnp.int32, (tm, HEAD_PAD), 1) < HEAD_DIM
    for c in range(3):
        y = _dot(h, wk_ref[:, c * BRANCH_WIDTH:(c + 1) * BRANCH_WIDTH])
        if c >= 1:
            y = _rope(y, ra_ref[...], rbm_ref[...], rbp_ref[...], ROT_DIM // 2, 1)
        for pair in range(2):
            both = y[:, pair * HEAD_PAD:(pair + 1) * HEAD_PAD]
            for odd, src in enumerate((both, pltpu.roll(both, HEAD_DIM, 1))):
                col = c * QK_PAD + (2 * pair + odd) * HEAD_PAD
                kpad_ref[:, col:col + HEAD_PAD] = jnp.where(low_half, src, 0.0).astype(BF16)

    t_all = _dot_nt(wt_ref[...], h)
    for c in range(6):
        rows = slice(c * BRANCH_WIDTH, (c + 1) * BRANCH_WIDTH)
        y = t_all[rows, :]
        if c >= 4:
            y = _rope(y, rat_ref[...], rbmt_ref[...], rbpt_ref[...], ROT_DIM // 2, 0)
        if c >= 3:
            y = y * (HEAD_DIM ** -0.5 * LOG2E)
        qvt_ref[0, rows, :] = y.astype(BF16)


def _inproj(x2, g, weights, rope_p, rope_mk, rope_mq):
    m = x2.shape[0]
    tm = TOKEN_TILE
    seq_tiles = SEQ // tm
    row = lambda i: (i, 0)
    by_batch = lambda i: (i // seq_tiles, 0, i % seq_tiles)
    seq_rows = lambda width: pl.BlockSpec((tm, width), lambda i: (i % seq_tiles, 0))
    seq_cols = lambda height: pl.BlockSpec((height, tm), lambda i: (0, i % seq_tiles))
    tables = ([seq_rows(BRANCH_WIDTH)] * 3 + [seq_cols(BRANCH_WIDTH)] * 3
              + [seq_rows(HEAD_PAD)] * 3 + [seq_cols(MLA_QK_DIM)] * 3)
    (g, *weights), whole_specs = _whole([g] + list(weights))
    return pl.pallas_call(
        _inproj_kernel,
        grid=(m // tm,),
        in_specs=[pl.BlockSpec((tm, D_MODEL), row)] + whole_specs + tables,
        out_specs=[pl.BlockSpec((tm, 4 * QK_PAD), row), pl.BlockSpec((1, 7 * BRANCH_WIDTH, tm), by_batch),
                   pl.BlockSpec((1, N_HEADS * MLA_QK_DIM, tm), by_batch), pl.BlockSpec((tm, HEAD_PAD), row)],
        out_shape=[jax.ShapeDtypeStruct((m, 4 * QK_PAD), BF16),
                   jax.ShapeDtypeStruct((m // SEQ, 7 * BRANCH_WIDTH, SEQ), BF16),
                   jax.ShapeDtypeStruct((m // SEQ, N_HEADS * MLA_QK_DIM, SEQ), BF16),
                   jax.ShapeDtypeStruct((m, HEAD_PAD), F32)],
        compiler_params=_params("parallel"),
        name="inproj",
    )(x2, g, *weights, *rope_p, *[t.T for t in rope_p], *rope_mk, *[t.T for t in rope_mq])


def _fox_bias_kernel(f_ref, bf_ref, to_k_ref, to_q_ref, q_ones_ref, k_ones_ref, qb_ref, kb_ref):
    chunk = 256
    ri = lax.broadcasted_iota(jnp.int32, (chunk, chunk), 0)
    ci = lax.broadcasted_iota(jnp.int32, (chunk, chunk), 1)
    tril = jnp.where(ri >= ci, 1.0, 0.0).astype(BF16)
    lane = lax.broadcasted_iota(jnp.int32, (chunk, HEAD_PAD), 1)
    carry = jnp.zeros((1, HEAD_PAD), F32)
    for c in range(SEQ // chunk):
        rows = slice(c * chunk, (c + 1) * chunk)
        z = f_ref[0, rows, :] + bf_ref[...]
        logf = jnp.minimum(z, 0.0) - jnp.log(1.0 + jnp.exp(-jnp.abs(z)))
        hi, mid, lo = _split3(logf)
        cum = _dot(tril, hi) + _dot(tril, mid) + _dot(tril, lo) + carry
        carry = cum[chunk - 1:chunk, :]
        c_hi, c_mid, c_lo = (p.astype(F32) for p in _split3(cum * LOG2E))
        packed = jnp.where(lane < N_HEADS, c_hi,
                           jnp.where(lane < 2 * N_HEADS, pltpu.roll(c_mid, N_HEADS, 1),
                                     pltpu.roll(c_lo, 2 * N_HEADS, 1))).astype(BF16)
        kb_ref[0, rows, :] = (k_ones_ref[...] - _dot(packed, to_k_ref[...])).astype(BF16)
        qb_ref[0, :, rows] = (_dot_nt(to_q_ref[...], packed) + q_ones_ref[...]).astype(BF16)


def _fox_bias(fpre3, bf, consts):
    b = fpre3.shape[0]
    (bf, *consts), whole_specs = _whole([bf] + list(consts))
    return pl.pallas_call(
        _fox_bias_kernel,
        grid=(b,),
        in_specs=[pl.BlockSpec((1, SEQ, HEAD_PAD), lambda i: (i, 0, 0))] + whole_specs,
        out_specs=[pl.BlockSpec((1, N_HEADS * BIAS_SLOTS, SEQ), lambda i: (i, 0, 0)),
                   pl.BlockSpec((1, SEQ, HEAD_PAD), lambda i: (i, 0, 0))],
        out_shape=[jax.ShapeDtypeStruct((b, N_HEADS * BIAS_SLOTS, SEQ), BF16),
                   jax.ShapeDtypeStruct((b, SEQ, HEAD_PAD), BF16)],
        compiler_params=_params("parallel"),
        name="fox_bias",
    )(fpre3, bf, *consts)


def _moba_bias_kernel(qt_ref, k_ref, ind_ref, hmask_ref, qb_ref):
    kmean = _dot(ind_ref[...], k_ref[0]) * (1.0 / MOBA_BLOCK) * hmask_ref[...]
    pieces = _split3(kmean)
    zeros = jnp.zeros((HEAD_PAD - HEAD_DIM, SEQ), BF16)
    qt_pad = jnp.concatenate(
        [part for h in range(N_HEADS) for part in (qt_ref[0, h * HEAD_DIM:(h + 1) * HEAD_DIM, :], zeros)], axis=0)
    gate = _dot(pieces[0], qt_pad) + _dot(pieces[1], qt_pad) + _dot(pieces[2], qt_pad)
    blk = lax.broadcasted_iota(jnp.int32, (N_MOBA_BLOCKS, SEQ), 0)
    own = lax.broadcasted_iota(jnp.int32, (N_MOBA_BLOCKS, SEQ), 1) // MOBA_BLOCK
    for h in range(N_HEADS):
        g = gate[h * N_MOBA_BLOCKS:(h + 1) * N_MOBA_BLOCKS, :]
        rank = jnp.zeros(g.shape, F32)
        for r in range(1, N_MOBA_BLOCKS):
            other = pltpu.roll(g, N_MOBA_BLOCKS - r, 0)
            other_blk = jnp.where(blk + r >= N_MOBA_BLOCKS, blk + r - N_MOBA_BLOCKS, blk + r)
            ahead = (other > g) | ((other == g) & (other_blk < blk))
            rank = rank + jnp.where(ahead & (other_blk < own), 1.0, 0.0)
        keep = ((blk < own) & (rank < MOBA_TOPK)) | (blk == own)
        bias = jnp.concatenate([jnp.where(keep, 0.0, NEG), jnp.zeros(g.shape, F32)], axis=0)
        qb_ref[0, h * BIAS_SLOTS:(h + 1) * BIAS_SLOTS, :] = bias.astype(BF16)


def _moba_bias(qvt, q_row, kpad3, k_col, ind, hmask):
    b = qvt.shape[0]
    return pl.pallas_call(
        _moba_bias_kernel,
        grid=(b,),
        in_specs=[pl.BlockSpec((1, BRANCH_WIDTH, SEQ), lambda i: (i, q_row, 0)),
                  pl.BlockSpec((1, SEQ, QK_PAD), lambda i: (i, 0, k_col)),
                  _full(ind.shape), _full(hmask.shape)],
        out_specs=pl.BlockSpec((1, N_HEADS * BIAS_SLOTS, SEQ), lambda i: (i, 0, 0)),
        out_shape=jax.ShapeDtypeStruct((b, N_HEADS * BIAS_SLOTS, SEQ), BF16),
        compiler_params=_params("parallel"),
        name="moba_bias",
    )(qvt, kpad3, ind, hmask)


def _attn_kernel(*refs, q_dim, biased, dilated):
    if biased:
        qt_ref, qb_ref, k_ref, kb_ref, vt_ref, tab_ref, o_ref, sa_ref, sb_ref, m_ref, acc_ref = refs
    else:
        qt_ref, k_ref, vt_ref, tab_ref, o_ref, sa_ref, sb_ref, m_ref, acc_ref = refs
    t = ATTN_TILE
    n_tiles = SEQ // t
    ones_rows = jnp.ones((ACC_ROWS - HEAD_DIM, t), BF16)

    m_ref[...] = jnp.full(m_ref.shape, NEG, F32)
    acc_ref[...] = jnp.zeros(acc_ref.shape, F32)

    def scores(i, j, s_ref):
        qo = pl.multiple_of(i * t, t)
        ko = pl.multiple_of(j * t, t)
        for h in range(N_HEADS):
            parts = [qt_ref[0, h * q_dim:(h + 1) * q_dim, pl.ds(qo, t)]]
            k_op = k_ref[0, pl.ds(ko, t), h * HEAD_PAD:(h + 1) * HEAD_PAD]
            if biased:
                k_op = k_op + kb_ref[0, pl.ds(ko, t), :]
                if h > 0:
                    parts.append(jnp.zeros((h * BIAS_SLOTS, t), BF16))
                parts.append(qb_ref[0, h * BIAS_SLOTS:(h + 1) * BIAS_SLOTS, pl.ds(qo, t)])
            fill = HEAD_PAD - sum(p.shape[0] for p in parts)
            if fill:
                parts.append(jnp.zeros((fill, t), BF16))
            s_ref[h] = _dot(k_op, jnp.concatenate(parts, axis=0))

    def update(i, j, s_ref, bias):
        qo = pl.multiple_of(i * t, t)
        ko = pl.multiple_of(j * t, t)
        for h in range(N_HEADS):
            s = s_ref[h] if bias is None else s_ref[h] + bias
            m_old = m_ref[h, :, pl.ds(qo, t)]
            m_new = jnp.maximum(m_old, jnp.max(s, axis=0, keepdims=True))
            m_ref[h, :, pl.ds(qo, t)] = m_new
            p = jnp.exp2(s - m_new).astype(BF16)
            v_aug = jnp.concatenate([vt_ref[0, h * HEAD_DIM:(h + 1) * HEAD_DIM, pl.ds(ko, t)], ones_rows], axis=0)
            rows = slice(h * ACC_ROWS, (h + 1) * ACC_ROWS)
            acc_ref[rows, pl.ds(qo, t)] = (acc_ref[rows, pl.ds(qo, t)] * jnp.exp2(m_old - m_new)
                                           + _dot(v_aug, p))

    def pipeline(start, succ, n_steps, per_trip, bias_of):
        def trip(_, ij):
            cur = ij
            for step in range(per_trip):
                s_cur, s_next = (sa_ref, sb_ref) if step % 2 == 0 else (sb_ref, sa_ref)
                nxt = succ(*cur)
                scores(*nxt, s_next)
                update(*cur, s_cur, bias_of(*cur))
                cur = nxt
            return cur

        assert n_steps % per_trip == 0 and per_trip % 2 == 0
        scores(*start, sa_ref)
        lax.fori_loop(0, n_steps // per_trip, trip, (jnp.int32(start[0]), jnp.int32(start[1])))

    def succ_diag(i, j):
        nxt = jnp.minimum(i + 1, n_tiles - 1)
        return nxt, nxt

    def succ_below(i, j):
        last = j == i - 1
        return jnp.where(last, jnp.minimum(i + 1, n_tiles - 1), i), jnp.where(last, 0, j + 1)

    pipeline((0, 0), succ_diag, n_tiles, DIAG_STEPS_PER_TRIP, lambda i, j: tab_ref[0])
    pipeline((1, 0), succ_below, n_tiles * (n_tiles - 1) // 2, BELOW_STEPS_PER_TRIP,
             (lambda i, j: tab_ref[i - j]) if dilated else (lambda i, j: None))

    for i in range(n_tiles):
        q_cols = slice(i * t, (i + 1) * t)
        heads = []
        for h in range(N_HEADS):
            denom = acc_ref[h * ACC_ROWS + HEAD_DIM:h * ACC_ROWS + HEAD_DIM + 1, q_cols]
            heads.append(acc_ref[h * ACC_ROWS:h * ACC_ROWS + HEAD_DIM, q_cols] * (1.0 / denom))
        o_ref[0, q_cols, :] = jnp.concatenate(heads, axis=0).T.astype(o_ref.dtype)


def _attention(qt, q_row, q_dim, k, k_col, vt, v_row, table, dilated=False, q_bias=None, k_bias=None):
    b = qt.shape[0]
    t = ATTN_TILE
    biased = q_bias is not None
    in_specs = [pl.BlockSpec((1, N_HEADS * q_dim, SEQ), lambda bi: (bi, q_row, 0))]
    args = [qt]
    if biased:
        in_specs.append(pl.BlockSpec((1, N_HEADS * BIAS_SLOTS, SEQ), lambda bi: (bi, 0, 0)))
        args.append(q_bias)
    in_specs.append(pl.BlockSpec((1, SEQ, QK_PAD), lambda bi: (bi, 0, k_col)))
    args.append(k)
    if biased:
        per_batch = k_bias.shape[0] > 1
        in_specs.append(pl.BlockSpec((1, SEQ, HEAD_PAD), lambda bi: (bi if per_batch else 0, 0, 0)))
        args.append(k_bias)
    in_specs += [pl.BlockSpec((1, BRANCH_WIDTH, SEQ), lambda bi: (bi, v_row, 0)), _full(table.shape)]
    args += [vt, table]
    return pl.pallas_call(
        functools.partial(_attn_kernel, q_dim=q_dim, biased=biased, dilated=dilated),
        grid=(b,),
        in_specs=in_specs,
        out_specs=pl.BlockSpec((1, SEQ, BRANCH_WIDTH), lambda bi: (bi, 0, 0)),
        out_shape=jax.ShapeDtypeStruct((b, SEQ, BRANCH_WIDTH), BF16),
        scratch_shapes=[pltpu.VMEM((N_HEADS, t, t), F32), pltpu.VMEM((N_HEADS, t, t), F32),
                        pltpu.VMEM((N_HEADS, 1, SEQ), F32), pltpu.VMEM((N_HEADS * ACC_ROWS, SEQ), F32)],
        compiler_params=_params("parallel"),
        name="attn_dil" if dilated else ("attn_bias" if biased else "attn"),
    )(*args)


def _merge_kernel(x_ref, gpre_ref, wg_ref, y0_ref, y1_ref, y2_ref, y3_ref, wb_ref, wo_ref, gpost_ref,
                  o_ref):
    x = x_ref[...]
    h = _rms(x, gpre_ref[...]).astype(BF16)
    mix = None
    for c in range(D_MODEL // MERGE_CHUNK):
        cols = slice(c * MERGE_CHUNK, (c + 1) * MERGE_CHUNK)
        merged = None
        for n, y_ref in enumerate((y0_ref, y1_ref, y2_ref, y3_ref)):
            gate = jax.nn.sigmoid(_dot(h, wg_ref[:, n * D_MODEL + c * MERGE_CHUNK:
                                                 n * D_MODEL + (c + 1) * MERGE_CHUNK]))
            term = gate * _dot(y_ref[...], wb_ref[n, :, cols])
            merged = term if merged is None else merged + term
        part = _dot(merged.astype(BF16), wo_ref[cols, :])
        mix = part if mix is None else mix + part
    o_ref[...] = x + _rms(mix, gpost_ref[...])


def _merge(x2, gpre, wg, ys, wb, wo, gpost):
    m = x2.shape[0]
    tm = TOKEN_TILE
    row = lambda i: (i, 0)
    yspec = pl.BlockSpec((tm, BRANCH_WIDTH), row)
    (gpre, wg, wb, wo, gpost), (s_gpre, s_wg, s_wb, s_wo, s_gpost) = _whole([gpre, wg, wb, wo, gpost])
    return pl.pallas_call(
        _merge_kernel,
        grid=(m // tm,),
        in_specs=[pl.BlockSpec((tm, D_MODEL), row), s_gpre, s_wg, yspec, yspec, yspec, yspec,
                  s_wb, s_wo, s_gpost],
        out_specs=pl.BlockSpec((tm, D_MODEL), row),
        out_shape=jax.ShapeDtypeStruct((m, D_MODEL), F32),
        compiler_params=_params("parallel"),
        name="merge",
    )(x2, gpre, wg, *ys, wb, wo, gpost)


def _mlp_kernel(x_ref, gpre_ref, wup_ref, wdown_ref, gpost_ref, o_ref):
    x = x_ref[...]
    h = _rms(x, gpre_ref[...]).astype(BF16)
    acc = None
    for c in range(D_FF // FF_CHUNK):
        cols = slice(c * FF_CHUNK, (c + 1) * FF_CHUNK)
        up = jnp.maximum(_dot(h, wup_ref[:, cols]), 0.0)
        term = _dot((up * up).astype(BF16), wdown_ref[cols, :])
        acc = term if acc is None else acc + term
    o_ref[...] = x + _rms(acc, gpost_ref[...])


def _mlp(x2, gpre, wup, wdown, gpost):
    m = x2.shape[0]
    tm = TOKEN_TILE
    row = lambda i: (i, 0)
    (gpre, wup, wdown, gpost), whole_specs = _whole([gpre, wup, wdown, gpost])
    return pl.pallas_call(
        _mlp_kernel,
        grid=(m // tm,),
        in_specs=[pl.BlockSpec((tm, D_MODEL), row)] + whole_specs,
        out_specs=pl.BlockSpec((tm, D_MODEL), row),
        out_shape=jax.ShapeDtypeStruct((m, D_MODEL), F32),
        compiler_params=_params("parallel"),
        name="mlp",
    )(x2, gpre, wup, wdown, gpost)


def kernel(x, w_in, b_forget, g_cq, g_ckv, w_uq, w_uk, w_uv, w_branch, w_out, w_up, w_down,
           g_pre_mix, g_post_mix, g_pre_mlp, g_post_mlp):
    b, s, d = x.shape
    assert (s, d) == (SEQ, D_MODEL)
    depth = w_in.shape[0]
    m = b * s
    rope_p, rope_mk, rope_mq = _rope_tables()
    fox_consts = _fox_constants()
    moba_ind, moba_hmask, moba_kb = _block_indicator(), _block_head_mask(), _moba_key_bias()
    dil_bias = _dilated_bias()
    causal_bias = _causal_bias()

    bw = BRANCH_WIDTH
    o_f = 3 * bw
    o_dil = o_f + N_HEADS
    o_moba = o_dil + 3 * bw
    o_cq = o_moba + 3 * bw
    o_ckv = o_cq + MLA_Q_RANK
    o_kr = o_ckv + MLA_KV_RANK
    o_gate = o_kr + MLA_ROPE_DIM

    qkv_at = (0, o_dil, o_moba)
    wk_all = jnp.concatenate([w_in[:, :, o + bw:o + 2 * bw] for o in qkv_at], axis=2).astype(BF16)
    wt_all = jnp.swapaxes(jnp.concatenate([w_in[:, :, o + 2 * bw:o + 3 * bw] for o in qkv_at]
                                          + [w_in[:, :, o:o + bw] for o in qkv_at], axis=2), 1, 2).astype(BF16)
    wsmall_all = jnp.concatenate(
        [w_in[:, :, o_cq:o_kr], jnp.zeros((depth, d, MLA_NOPE_DIM), F32), w_in[:, :, o_kr:o_gate],
         jnp.zeros((depth, d, HEAD_PAD - MLA_QK_DIM), F32)], axis=2).astype(BF16)
    wf_all = jnp.concatenate([jnp.swapaxes(w_in[:, :, o_f:o_dil], 1, 2),
                              jnp.zeros((depth, 8 - N_HEADS, d), F32)], axis=1)
    wgate_all = w_in[:, :, o_gate:].astype(BF16)
    bf_all = jnp.concatenate([b_forget, jnp.zeros((depth, HEAD_PAD - N_HEADS), F32)], axis=1)
    wuqt_all = jnp.swapaxes(w_uq, 1, 2).astype(BF16)
    wuk_all = jnp.concatenate(
        [part for h in range(N_HEADS) for part in (w_uk[:, :, h * MLA_NOPE_DIM:(h + 1) * MLA_NOPE_DIM],
                                                   jnp.zeros((depth, MLA_KV_RANK, HEAD_PAD - MLA_NOPE_DIM), F32))],
        axis=2).astype(BF16)
    wuvt_all = jnp.swapaxes(w_uv, 1, 2).astype(BF16)
    wb_all, wo_all = w_branch.astype(BF16), w_out.astype(BF16)
    wup_all, wdown_all = w_up.astype(BF16), w_down.astype(BF16)

    rows = lambda g: g.reshape(depth, 1, -1)
    g_cq, g_ckv, bf_all = rows(g_cq), rows(g_ckv), rows(bf_all)
    g_pre_mix, g_post_mix, g_pre_mlp, g_post_mlp = (rows(g) for g in (g_pre_mix, g_post_mix, g_pre_mlp,
                                                                       g_post_mlp))

    x2 = x.reshape(m, d)
    for l in range(depth):
        weights = [(w, l) for w in (wk_all, wt_all, wsmall_all, wf_all, g_cq, g_ckv, wuqt_all, wuk_all,
                                    wuvt_all)]
        kpad, qvt, lqt, fpre = _inproj(x2, (g_pre_mix, l), weights, rope_p, rope_mk, rope_mq)
        kpad3 = kpad.reshape(b, s, 4 * QK_PAD)
        fox_qb, fox_kb = _fox_bias(fpre.reshape(b, s, HEAD_PAD), (bf_all, l), fox_consts)
        moba_qb = _moba_bias(qvt, 5, kpad3, 2, moba_ind, moba_hmask)
        y_fox = _attention(qvt, 3, HEAD_DIM, kpad3, 0, qvt, 0, causal_bias, q_bias=fox_qb, k_bias=fox_kb)
        y_dil = _attention(qvt, 4, HEAD_DIM, kpad3, 1, qvt, 1, dil_bias, dilated=True)
        y_moba = _attention(qvt, 5, HEAD_DIM, kpad3, 2, qvt, 2, causal_bias, q_bias=moba_qb, k_bias=moba_kb)
        y_mla = _attention(lqt, 0, MLA_QK_DIM, kpad3, 3, qvt, 6, causal_bias)
        ys = [y.reshape(m, BRANCH_WIDTH) for y in (y_fox, y_dil, y_moba, y_mla)]
        x2 = _merge(x2, (g_pre_mix, l), (wgate_all, l), ys, (wb_all, l), (wo_all, l), (g_post_mix, l))
        x2 = _mlp(x2, (g_pre_mlp, l), (wup_all, l), (wdown_all, l), (g_post_mlp, l))
    return x2.reshape(b, s, d)
```

```python
import functools

import numpy as np
import jax
import jax.numpy as jnp
from jax import lax
from jax.experimental import pallas as pl
from jax.experimental.pallas import tpu as pltpu

LANES = 128
F32_SUBLANES = 8
BF16_SUBLANES = 16
V7X_VMEM_BYTES = 64 * 1024 * 1024

D_MODEL = 1024
SEQ = 2048
HEAD_DIM = 64
N_HEADS = 4
BRANCH_WIDTH = N_HEADS * HEAD_DIM
HEAD_PAD = LANES
QK_PAD = N_HEADS * HEAD_PAD
BIAS_SLOTS = BF16_SUBLANES
ROT_DIM = HEAD_DIM // 4
ROPE_THETA = 500000.0
DILATED_PAIRS = ((128, 1), (512, 4), (2048, 16))
MOBA_BLOCK = 256
MOBA_TOPK = 3
N_MOBA_BLOCKS = SEQ // MOBA_BLOCK
MLA_Q_RANK = 256
MLA_KV_RANK = 128
MLA_NOPE_DIM = 64
MLA_ROPE_DIM = 32
MLA_QK_DIM = MLA_NOPE_DIM + MLA_ROPE_DIM
D_FF = 4 * D_MODEL
NORM_EPS = 1e-6
NEG = -1e30
LOG2E = 1.4426950408889634

TOKEN_TILE = 512
ATTN_TILE = 256
N_ATTN_TILES = SEQ // ATTN_TILE
ACC_ROWS = HEAD_DIM + BF16_SUBLANES
DIAG_STEPS_PER_TRIP = N_ATTN_TILES
BELOW_STEPS_PER_TRIP = N_ATTN_TILES * (N_ATTN_TILES - 1) // 4
FF_CHUNK = 512
MERGE_CHUNK = 256
VMEM_LIMIT = V7X_VMEM_BYTES * 3 // 4

F32 = jnp.float32
BF16 = jnp.bfloat16


def _dot(a, b):
    return jnp.dot(a, b, preferred_element_type=F32)


def _dot_nt(a, b):
    return lax.dot_general(a, b, (((1,), (1,)), ((), ())), preferred_element_type=F32)


def _rms(x, g):
    return x * lax.rsqrt(jnp.mean(x * x, axis=-1, keepdims=True) + NORM_EPS) * g


def _rope(y, a, bm, bp, half, axis):
    n = y.shape[axis]
    return y * a + pltpu.roll(y, n - half, axis) * bm + pltpu.roll(y, half, axis) * bp


def _split3(x):
    hi = x.astype(BF16)
    r = x - hi.astype(F32)
    mid = r.astype(BF16)
    lo = (r - mid.astype(F32)).astype(BF16)
    return hi, mid, lo


def _rope_tables():
    def tables(dim):
        inv_freq = 1.0 / (ROPE_THETA ** (jnp.arange(0, dim, 2, dtype=F32) / dim))
        ang = jnp.arange(SEQ, dtype=F32)[:, None] * inv_freq[None, :]
        return jnp.cos(ang), jnp.sin(ang)

    def expand(cos, sin, width, period, start, half):
        pos = np.arange(width) % period - start
        first = (pos >= 0) & (pos < half)
        second = (pos >= half) & (pos < 2 * half)
        idx = np.where(first, pos, np.where(second, pos - half, 0))
        a = jnp.where((first | second)[None, :], cos[:, idx], 1.0)
        bm = jnp.where(first[None, :], -sin[:, idx], 0.0)
        bp = jnp.where(second[None, :], sin[:, idx], 0.0)
        return a, bm, bp

    cos_p, sin_p = tables(ROT_DIM)
    cos_m, sin_m = tables(MLA_ROPE_DIM)
    partial = expand(cos_p, sin_p, BRANCH_WIDTH, HEAD_DIM, 0, ROT_DIM // 2)
    mla_k = expand(cos_m, sin_m, HEAD_PAD, HEAD_PAD, MLA_NOPE_DIM, MLA_ROPE_DIM // 2)
    mla_q = expand(cos_m, sin_m, MLA_QK_DIM, MLA_QK_DIM, MLA_NOPE_DIM, MLA_ROPE_DIM // 2)
    return partial, mla_k, mla_q


def _fox_constants():
    to_k = np.zeros((HEAD_PAD, HEAD_PAD), np.float32)
    to_q = np.zeros((N_HEADS * BIAS_SLOTS, HEAD_PAD), np.float32)
    q_ones = np.zeros((N_HEADS * BIAS_SLOTS, 1), np.float32)
    k_ones = np.zeros((1, HEAD_PAD), np.float32)
    for h in range(N_HEADS):
        for j in range(3):
            to_k[4 * j + h, HEAD_DIM + BIAS_SLOTS * h + 3 + j] = 1.0
            to_q[BIAS_SLOTS * h + j, 4 * j + h] = 1.0
        q_ones[BIAS_SLOTS * h + 3:BIAS_SLOTS * h + 6] = 1.0
        k_ones[0, HEAD_DIM + BIAS_SLOTS * h:HEAD_DIM + BIAS_SLOTS * h + 3] = 1.0
    return [jnp.asarray(to_k, BF16), jnp.asarray(to_q, BF16), jnp.asarray(q_ones), jnp.asarray(k_ones)]


def _moba_key_bias():
    out = np.zeros((1, SEQ, HEAD_PAD), np.float32)
    blk = np.arange(SEQ) // MOBA_BLOCK
    for h in range(N_HEADS):
        for n in range(N_MOBA_BLOCKS):
            out[0, :, HEAD_DIM + BIAS_SLOTS * h + n] = blk == n
    return jnp.asarray(out, BF16)


def _block_indicator():
    ind = np.zeros((HEAD_PAD, SEQ), np.float32)
    for h in range(N_HEADS):
        for n in range(N_MOBA_BLOCKS):
            ind[N_MOBA_BLOCKS * h + n, n * MOBA_BLOCK:(n + 1) * MOBA_BLOCK] = 1.0
    return jnp.asarray(ind, BF16)


def _block_head_mask():
    m = np.zeros((HEAD_PAD, QK_PAD), np.float32)
    for h in range(N_HEADS):
        m[N_MOBA_BLOCKS * h:N_MOBA_BLOCKS * (h + 1), h * HEAD_PAD:h * HEAD_PAD + HEAD_DIM] = 1.0
    return jnp.asarray(m)


def _causal_bias():
    key = np.arange(ATTN_TILE)[:, None]
    query = np.arange(ATTN_TILE)[None, :]
    return jnp.asarray(np.where(key <= query, 0.0, NEG).astype(np.float32)[None])


def _dilated_bias():
    r = np.arange(ATTN_TILE)[None, :]
    c = np.arange(ATTN_TILE)[:, None]
    count = np.zeros((SEQ // ATTN_TILE, ATTN_TILE, ATTN_TILE), np.float32)
    for delta in range(SEQ // ATTN_TILE):
        d = delta * ATTN_TILE + r - c
        for window, dil in DILATED_PAIRS:
            count[delta] += (d >= 0) & (d <= window) & (d % dil == 0)
    return jnp.asarray(np.where(count > 0, np.log2(np.maximum(count, 1.0)), NEG).astype(np.float32))


def _full(shape):
    return pl.BlockSpec(shape, lambda *_: (0,) * len(shape), pipeline_mode=pl.Buffered(1))


def _whole(operands):
    arrays, specs = [], []
    for op in operands:
        if isinstance(op, tuple):
            stacked, layer = op
            index = (layer,) + (0,) * (stacked.ndim - 1)
            arrays.append(stacked)
            specs.append(pl.BlockSpec((None,) + stacked.shape[1:], lambda *_, index=index: index,
                                      pipeline_mode=pl.Buffered(1)))
        else:
            arrays.append(op)
            specs.append(_full(op.shape))
    return arrays, specs


def _params(*sem):
    return pltpu.CompilerParams(dimension_semantics=sem, vmem_limit_bytes=VMEM_LIMIT)


def _inproj_kernel(x_ref, g_ref, wk_ref, wt_ref, wsmall_ref, wf_ref, gq_ref, gkv_ref, wuqt_ref, wuk_ref,
                   wuvt_ref, ra_ref, rbm_ref, rbp_ref, rat_ref, rbmt_ref, rbpt_ref, mra_ref, mrbm_ref,
                   mrbp_ref, mrat_ref, mrbmt_ref, mrbpt_ref, kpad_ref, qvt_ref, lqt_ref, f_ref):
    hf = _rms(x_ref[...], g_ref[...])
    h = hf.astype(BF16)
    tm = h.shape[0]

    lane = lax.broadcasted_iota(jnp.int32, f_ref.shape, 1)
    f = jnp.zeros(f_ref.shape, F32)
    for j in range(N_HEADS):
        col = jnp.sum(hf * wf_ref[j:j + 1, :], axis=-1, keepdims=True)
        f = jnp.where(lane == j, col, f)
    f_ref[...] = f

    small = _dot(h, wsmall_ref[...])
    half = MLA_ROPE_DIM // 2
    cq = _rms(small[:, 0:MLA_Q_RANK], gq_ref[...]).astype(BF16)
    ckv = _rms(small[:, MLA_Q_RANK:MLA_Q_RANK + MLA_KV_RANK], gkv_ref[...]).astype(BF16)
    kr = _rope(small[:, MLA_Q_RANK + MLA_KV_RANK:], mra_ref[...], mrbm_ref[...], mrbp_ref[...], half, 1)
    qf_t = _dot_nt(wuqt_ref[...], cq)
    kf = _dot(ckv, wuk_ref[...])
    for hd in range(N_HEADS):
        rows = slice(hd * MLA_QK_DIM, (hd + 1) * MLA_QK_DIM)
        q_h = _rope(qf_t[rows, :], mrat_ref[...], mrbmt_ref[...], mrbpt_ref[...], half, 0)
        lqt_ref[0, rows, :] = (q_h * (MLA_QK_DIM ** -0.5 * LOG2E)).astype(BF16)
        col = 3 * QK_PAD + hd * HEAD_PAD
        kpad_ref[:, col:col + HEAD_PAD] = (kf[:, hd * HEAD_PAD:(hd + 1) * HEAD_PAD] + kr).astype(BF16)
    qvt_ref[0, 6 * BRANCH_WIDTH:, :] = _dot_nt(wuvt_ref[...], ckv).astype(BF16)

    low_half = lax.broadcasted_iota(jnp.int32, (tm, HEAD_PAD), 1) < HEAD_DIM
    for c in range(3):
        y = _dot(h, wk_ref[:, c * BRANCH_WIDTH:(c + 1) * BRANCH_WIDTH])
        if c >= 1:
            y = _rope(y, ra_ref[...], rbm_ref[...], rbp_ref[...], ROT_DIM // 2, 1)
        for pair in range(2):
            both = y[:, pair * HEAD_PAD:(pair + 1) * HEAD_PAD]
            for odd, src in enumerate((both, pltpu.roll(both, HEAD_DIM, 1))):
                col = c * QK_PAD + (2 * pair + odd) * HEAD_PAD
                kpad_ref[:, col:col + HEAD_PAD] = jnp.where(low_half, src, 0.0).astype(BF16)

    t_all = _dot_nt(wt_ref[...], h)
    for c in range(6):
        rows = slice(c * BRANCH_WIDTH, (c + 1) * BRANCH_WIDTH)
        y = t_all[rows, :]
        if c >= 4:
            y = _rope(y, rat_ref[...], rbmt_ref[...], rbpt_ref[...], ROT_DIM // 2, 0)
        if c >= 3:
            y = y * (HEAD_DIM ** -0.5 * LOG2E)
        qvt_ref[0, rows, :] = y.astype(BF16)


def _inproj(x2, g, weights, rope_p, rope_mk, rope_mq):
    m = x2.shape[0]
    tm = TOKEN_TILE
    seq_tiles = SEQ // tm
    row = lambda i: (i, 0)
    by_batch = lambda i: (i // seq_tiles, 0, i % seq_tiles)
    seq_rows = lambda width: pl.BlockSpec((tm, width), lambda i: (i % seq_tiles, 0))
    seq_cols = lambda height: pl.BlockSpec((height, tm), lambda i: (0, i % seq_tiles))
    tables = ([seq_rows(BRANCH_WIDTH)] * 3 + [seq_cols(BRANCH_WIDTH)] * 3
              + [seq_rows(HEAD_PAD)] * 3 + [seq_cols(MLA_QK_DIM)] * 3)
    (g, *weights), whole_specs = _whole([g] + list(weights))
    return pl.pallas_call(
        _inproj_kernel,
        grid=(m // tm,),
        in_specs=[pl.BlockSpec((tm, D_MODEL), row)] + whole_specs + tables,
        out_specs=[pl.BlockSpec((tm, 4 * QK_PAD), row), pl.BlockSpec((1, 7 * BRANCH_WIDTH, tm), by_batch),
                   pl.BlockSpec((1, N_HEADS * MLA_QK_DIM, tm), by_batch), pl.BlockSpec((tm, HEAD_PAD), row)],
        out_shape=[jax.ShapeDtypeStruct((m, 4 * QK_PAD), BF16),
                   jax.ShapeDtypeStruct((m // SEQ, 7 * BRANCH_WIDTH, SEQ), BF16),
                   jax.ShapeDtypeStruct((m // SEQ, N_HEADS * MLA_QK_DIM, SEQ), BF16),
                   jax.ShapeDtypeStruct((m, HEAD_PAD), F32)],
        compiler_params=_params("parallel"),
        name="inproj",
    )(x2, g, *weights, *rope_p, *[t.T for t in rope_p], *rope_mk, *[t.T for t in rope_mq])


def _fox_bias_kernel(f_ref, bf_ref, to_k_ref, to_q_ref, q_ones_ref, k_ones_ref, qb_ref, kb_ref):
    chunk = 256
    ri = lax.broadcasted_iota(jnp.int32, (chunk, chunk), 0)
    ci = lax.broadcasted_iota(jnp.int32, (chunk, chunk), 1)
    tril = jnp.where(ri >= ci, 1.0, 0.0).astype(BF16)
    lane = lax.broadcasted_iota(jnp.int32, (chunk, HEAD_PAD), 1)
    carry = jnp.zeros((1, HEAD_PAD), F32)
    for c in range(SEQ // chunk):
        rows = slice(c * chunk, (c + 1) * chunk)
        z = f_ref[0, rows, :] + bf_ref[...]
        logf = jnp.minimum(z, 0.0) - jnp.log(1.0 + jnp.exp(-jnp.abs(z)))
        hi, mid, lo = _split3(logf)
        cum = _dot(tril, hi) + _dot(tril, mid) + _dot(tril, lo) + carry
        carry = cum[chunk - 1:chunk, :]
        c_hi, c_mid, c_lo = (p.astype(F32) for p in _split3(cum * LOG2E))
        packed = jnp.where(lane < N_HEADS, c_hi,
                           jnp.where(lane < 2 * N_HEADS, pltpu.roll(c_mid, N_HEADS, 1),
                                     pltpu.roll(c_lo, 2 * N_HEADS, 1))).astype(BF16)
        kb_ref[0, rows, :] = (k_ones_ref[...] - _dot(packed, to_k_ref[...])).astype(BF16)
        qb_ref[0, :, rows] = (_dot_nt(to_q_ref[...], packed) + q_ones_ref[...]).astype(BF16)


def _fox_bias(fpre3, bf, consts):
    b = fpre3.shape[0]
    (bf, *consts), whole_specs = _whole([bf] + list(consts))
    return pl.pallas_call(
        _fox_bias_kernel,
        grid=(b,),
        in_specs=[pl.BlockSpec((1, SEQ, HEAD_PAD), lambda i: (i, 0, 0))] + whole_specs,
        out_specs=[pl.BlockSpec((1, N_HEADS * BIAS_SLOTS, SEQ), lambda i: (i, 0, 0)),
                   pl.BlockSpec((1, SEQ, HEAD_PAD), lambda i: (i, 0, 0))],
        out_shape=[jax.ShapeDtypeStruct((b, N_HEADS * BIAS_SLOTS, SEQ), BF16),
                   jax.ShapeDtypeStruct((b, SEQ, HEAD_PAD), BF16)],
        compiler_params=_params("parallel"),
        name="fox_bias",
    )(fpre3, bf, *consts)


def _moba_bias_kernel(qt_ref, k_ref, ind_ref, hmask_ref, qb_ref):
    kmean = _dot(ind_ref[...], k_ref[0]) * (1.0 / MOBA_BLOCK) * hmask_ref[...]
    pieces = _split3(kmean)
    zeros = jnp.zeros((HEAD_PAD - HEAD_DIM, SEQ), BF16)
    qt_pad = jnp.concatenate(
        [part for h in range(N_HEADS) for part in (qt_ref[0, h * HEAD_DIM:(h + 1) * HEAD_DIM, :], zeros)], axis=0)
    gate = _dot(pieces[0], qt_pad) + _dot(pieces[1], qt_pad) + _dot(pieces[2], qt_pad)
    blk = lax.broadcasted_iota(jnp.int32, (N_MOBA_BLOCKS, SEQ), 0)
    own = lax.broadcasted_iota(jnp.int32, (N_MOBA_BLOCKS, SEQ), 1) // MOBA_BLOCK
    for h in range(N_HEADS):
        g = gate[h * N_MOBA_BLOCKS:(h + 1) * N_MOBA_BLOCKS, :]
        rank = jnp.zeros(g.shape, F32)
        for r in range(1, N_MOBA_BLOCKS):
            other = pltpu.roll(g, N_MOBA_BLOCKS - r, 0)
            other_blk = jnp.where(blk + r >= N_MOBA_BLOCKS, blk + r - N_MOBA_BLOCKS, blk + r)
            ahead = (other > g) | ((other == g) & (other_blk < blk))
            rank = rank + jnp.where(ahead & (other_blk < own), 1.0, 0.0)
        keep = ((blk < own) & (rank < MOBA_TOPK)) | (blk == own)
        bias = jnp.concatenate([jnp.where(keep, 0.0, NEG), jnp.zeros(g.shape, F32)], axis=0)
        qb_ref[0, h * BIAS_SLOTS:(h + 1) * BIAS_SLOTS, :] = bias.astype(BF16)


def _moba_bias(qvt, q_row, kpad3, k_col, ind, hmask):
    b = qvt.shape[0]
    return pl.pallas_call(
        _moba_bias_kernel,
        grid=(b,),
        in_specs=[pl.BlockSpec((1, BRANCH_WIDTH, SEQ), lambda i: (i, q_row, 0)),
                  pl.BlockSpec((1, SEQ, QK_PAD), lambda i: (i, 0, k_col)),
                  _full(ind.shape), _full(hmask.shape)],
        out_specs=pl.BlockSpec((1, N_HEADS * BIAS_SLOTS, SEQ), lambda i: (i, 0, 0)),
        out_shape=jax.ShapeDtypeStruct((b, N_HEADS * BIAS_SLOTS, SEQ), BF16),
        compiler_params=_params("parallel"),
        name="moba_bias",
    )(qvt, kpad3, ind, hmask)


def _attn_kernel(*refs, q_dim, biased, dilated):
    if biased:
        qt_ref, qb_ref, k_ref, kb_ref, vt_ref, tab_ref, o_ref, sa_ref, sb_ref, m_ref, acc_ref = refs
    else:
        qt_ref, k_ref, vt_ref, tab_ref, o_ref, sa_ref, sb_ref, m_ref, acc_ref = refs
    t = ATTN_TILE
    n_tiles = N_ATTN_TILES
    ones_rows = jnp.ones((ACC_ROWS - HEAD_DIM, t), BF16)

    m_ref[...] = jnp.full(m_ref.shape, NEG, F32)
    acc_ref[...] = jnp.zeros(acc_ref.shape, F32)

    def scores(i, j, s_ref):
        qo = pl.multiple_of(i * t, t)
        ko = pl.multiple_of(j * t, t)
        for h in range(N_HEADS):
            parts = [qt_ref[0, h * q_dim:(h + 1) * q_dim, pl.ds(qo, t)]]
            k_op = k_ref[0, pl.ds(ko, t), h * HEAD_PAD:(h + 1) * HEAD_PAD]
            if biased:
                k_op = k_op + kb_ref[0, pl.ds(ko, t), :]
                if h > 0:
                    parts.append(jnp.zeros((h * BIAS_SLOTS, t), BF16))
                parts.append(qb_ref[0, h * BIAS_SLOTS:(h + 1) * BIAS_SLOTS, pl.ds(qo, t)])
            fill = HEAD_PAD - sum(p.shape[0] for p in parts)
            if fill:
                parts.append(jnp.zeros((fill, t), BF16))
            s_ref[h] = _dot(k_op, jnp.concatenate(parts, axis=0))

    def update(i, j, s_ref, bias):
        qo = pl.multiple_of(i * t, t)
        ko = pl.multiple_of(j * t, t)
        for h in range(N_HEADS):
            s = s_ref[h] if bias is None else s_ref[h] + bias
            m_old = m_ref[h, :, pl.ds(qo, t)]
            m_new = jnp.maximum(m_old, jnp.max(s, axis=0, keepdims=True))
            m_ref[h, :, pl.ds(qo, t)] = m_new
            p = jnp.exp2(s - m_new).astype(BF16)
            v_aug = jnp.concatenate([vt_ref[0, h * HEAD_DIM:(h + 1) * HEAD_DIM, pl.ds(ko, t)], ones_rows], axis=0)
            rows = slice(h * ACC_ROWS, (h + 1) * ACC_ROWS)
            acc_ref[rows, pl.ds(qo, t)] = (acc_ref[rows, pl.ds(qo, t)] * jnp.exp2(m_old - m_new)
                                           + _dot(v_aug, p))

    def pipeline(start, succ, n_steps, per_trip, bias_of):
        def trip(_, ij):
            cur = ij
            for step in range(per_trip):
                s_cur, s_next = (sa_ref, sb_ref) if step % 2 == 0 else (sb_ref, sa_ref)
                nxt = succ(*cur)
                scores(*nxt, s_next)
                update(*cur, s_cur, bias_of(*cur))
                cur = nxt
            return cur

        assert n_steps % per_trip == 0 and per_trip % 2 == 0
        scores(*start, sa_ref)
        lax.fori_loop(0, n_steps // per_trip, trip, (jnp.int32(start[0]), jnp.int32(start[1])))

    def succ_diag(i, j):
        nxt = jnp.minimum(i + 1, n_tiles - 1)
        return nxt, nxt

    def succ_below(i, j):
        last = j == i - 1
        return jnp.where(last, jnp.minimum(i + 1, n_tiles - 1), i), jnp.where(last, 0, j + 1)

    pipeline((0, 0), succ_diag, n_tiles, DIAG_STEPS_PER_TRIP, lambda i, j: tab_ref[0])
    pipeline((1, 0), succ_below, n_tiles * (n_tiles - 1) // 2, BELOW_STEPS_PER_TRIP,
             (lambda i, j: tab_ref[i - j]) if dilated else (lambda i, j: None))

    for i in range(n_tiles):
        q_cols = slice(i * t, (i + 1) * t)
        heads = []
        for h in range(N_HEADS):
            denom = acc_ref[h * ACC_ROWS + HEAD_DIM:h * ACC_ROWS + HEAD_DIM + 1, q_cols]
            heads.append(acc_ref[h * ACC_ROWS:h * ACC_ROWS + HEAD_DIM, q_cols] * (1.0 / denom))
        o_ref[0, q_cols, :] = jnp.concatenate(heads, axis=0).T.astype(o_ref.dtype)


def _attention(qt, q_row, q_dim, k, k_col, vt, v_row, table, dilated=False, q_bias=None, k_bias=None):
    b = qt.shape[0]
    t = ATTN_TILE
    biased = q_bias is not None
    in_specs = [pl.BlockSpec((1, N_HEADS * q_dim, SEQ), lambda bi: (bi, q_row, 0))]
    args = [qt]
    if biased:
        in_specs.append(pl.BlockSpec((1, N_HEADS * BIAS_SLOTS, SEQ), lambda bi: (bi, 0, 0)))
        args.append(q_bias)
    in_specs.append(pl.BlockSpec((1, SEQ, QK_PAD), lambda bi: (bi, 0, k_col)))
    args.append(k)
    if biased:
        per_batch = k_bias.shape[0] > 1
        in_specs.append(pl.BlockSpec((1, SEQ, HEAD_PAD), lambda bi: (bi if per_batch else 0, 0, 0)))
        args.append(k_bias)
    in_specs += [pl.BlockSpec((1, BRANCH_WIDTH, SEQ), lambda bi: (bi, v_row, 0)), _full(table.shape)]
    args += [vt, table]
    return pl.pallas_call(
        functools.partial(_attn_kernel, q_dim=q_dim, biased=biased, dilated=dilated),
        grid=(b,),
        in_specs=in_specs,
        out_specs=pl.BlockSpec((1, SEQ, BRANCH_WIDTH), lambda bi: (bi, 0, 0)),
        out_shape=jax.ShapeDtypeStruct((b, SEQ, BRANCH_WIDTH), BF16),
        scratch_shapes=[pltpu.VMEM((N_HEADS, t, t), F32), pltpu.VMEM((N_HEADS, t, t), F32),
                        pltpu.VMEM((N_HEADS, 1, SEQ), F32), pltpu.VMEM((N_HEADS * ACC_ROWS, SEQ), F32)],
        compiler_params=_params("parallel"),
        name="attn_dil" if dilated else ("attn_bias" if biased else "attn"),
    )(*args)


def _merge_kernel(x_ref, gpre_ref, wg_ref, y0_ref, y1_ref, y2_ref, y3_ref, wb_ref, wo_ref, gpost_ref,
                  o_ref):
    x = x_ref[...]
    h = _rms(x, gpre_ref[...]).astype(BF16)
    mix = None
    for c in range(D_MODEL // MERGE_CHUNK):
        cols = slice(c * MERGE_CHUNK, (c + 1) * MERGE_CHUNK)
        merged = None
        for n, y_ref in enumerate((y0_ref, y1_ref, y2_ref, y3_ref)):
            gate = jax.nn.sigmoid(_dot(h, wg_ref[:, n * D_MODEL + c * MERGE_CHUNK:
                                                 n * D_MODEL + (c + 1) * MERGE_CHUNK]))
            term = gate * _dot(y_ref[...], wb_ref[n, :, cols])
            merged = term if merged is None else merged + term
        part = _dot(merged.astype(BF16), wo_ref[cols, :])
        mix = part if mix is None else mix + part
    o_ref[...] = x + _rms(mix, gpost_ref[...])


def _merge(x2, gpre, wg, ys, wb, wo, gpost):
    m = x2.shape[0]
    tm = TOKEN_TILE
    row = lambda i: (i, 0)
    yspec = pl.BlockSpec((tm, BRANCH_WIDTH), row)
    (gpre, wg, wb, wo, gpost), (s_gpre, s_wg, s_wb, s_wo, s_gpost) = _whole([gpre, wg, wb, wo, gpost])
    return pl.pallas_call(
        _merge_kernel,
        grid=(m // tm,),
        in_specs=[pl.BlockSpec((tm, D_MODEL), row), s_gpre, s_wg, yspec, yspec, yspec, yspec,
                  s_wb, s_wo, s_gpost],
        out_specs=pl.BlockSpec((tm, D_MODEL), row),
        out_shape=jax.ShapeDtypeStruct((m, D_MODEL), F32),
        compiler_params=_params("parallel"),
        name="merge",
    )(x2, gpre, wg, *ys, wb, wo, gpost)


def _mlp_kernel(x_ref, gpre_ref, wup_ref, wdown_ref, gpost_ref, o_ref):
    x = x_ref[...]
    h = _rms(x, gpre_ref[...]).astype(BF16)
    acc = None
    for c in range(D_FF // FF_CHUNK):
        cols = slice(c * FF_CHUNK, (c + 1) * FF_CHUNK)
        up = jnp.maximum(_dot(h, wup_ref[:, cols]), 0.0)
        term = _dot((up * up).astype(BF16), wdown_ref[cols, :])
        acc = term if acc is None else acc + term
    o_ref[...] = x + _rms(acc, gpost_ref[...])


def _mlp(x2, gpre, wup, wdown, gpost):
    m = x2.shape[0]
    tm = TOKEN_TILE
    row = lambda i: (i, 0)
    (gpre, wup, wdown, gpost), whole_specs = _whole([gpre, wup, wdown, gpost])
    return pl.pallas_call(
        _mlp_kernel,
        grid=(m // tm,),
        in_specs=[pl.BlockSpec((tm, D_MODEL), row)] + whole_specs,
        out_specs=pl.BlockSpec((tm, D_MODEL), row),
        out_shape=jax.ShapeDtypeStruct((m, D_MODEL), F32),
        compiler_params=_params("parallel"),
        name="mlp",
    )(x2, gpre, wup, wdown, gpost)


def kernel(x, w_in, b_forget, g_cq, g_ckv, w_uq, w_uk, w_uv, w_branch, w_out, w_up, w_down,
           g_pre_mix, g_post_mix, g_pre_mlp, g_post_mlp):
    b, s, d = x.shape
    assert (s, d) == (SEQ, D_MODEL)
    depth = w_in.shape[0]
    m = b * s
    rope_p, rope_mk, rope_mq = _rope_tables()
    fox_consts = _fox_constants()
    moba_ind, moba_hmask, moba_kb = _block_indicator(), _block_head_mask(), _moba_key_bias()
    dil_bias = _dilated_bias()
    causal_bias = _causal_bias()

    bw = BRANCH_WIDTH
    o_f = 3 * bw
    o_dil = o_f + N_HEADS
    o_moba = o_dil + 3 * bw
    o_cq = o_moba + 3 * bw
    o_ckv = o_cq + MLA_Q_RANK
    o_kr = o_ckv + MLA_KV_RANK
    o_gate = o_kr + MLA_ROPE_DIM

    qkv_at = (0, o_dil, o_moba)
    wk_all = jnp.concatenate([w_in[:, :, o + bw:o + 2 * bw] for o in qkv_at], axis=2).astype(BF16)
    wt_all = jnp.swapaxes(jnp.concatenate([w_in[:, :, o + 2 * bw:o + 3 * bw] for o in qkv_at]
                                          + [w_in[:, :, o:o + bw] for o in qkv_at], axis=2), 1, 2).astype(BF16)
    wsmall_all = jnp.concatenate(
        [w_in[:, :, o_cq:o_kr], jnp.zeros((depth, d, MLA_NOPE_DIM), F32), w_in[:, :, o_kr:o_gate],
         jnp.zeros((depth, d, HEAD_PAD - MLA_QK_DIM), F32)], axis=2).astype(BF16)
    wf_all = jnp.concatenate([jnp.swapaxes(w_in[:, :, o_f:o_dil], 1, 2),
                              jnp.zeros((depth, F32_SUBLANES - N_HEADS, d), F32)], axis=1)
    wgate_all = w_in[:, :, o_gate:].astype(BF16)
    bf_all = jnp.concatenate([b_forget, jnp.zeros((depth, HEAD_PAD - N_HEADS), F32)], axis=1)
    wuqt_all = jnp.swapaxes(w_uq, 1, 2).astype(BF16)
    wuk_all = jnp.concatenate(
        [part for h in range(N_HEADS) for part in (w_uk[:, :, h * MLA_NOPE_DIM:(h + 1) * MLA_NOPE_DIM],
                                                   jnp.zeros((depth, MLA_KV_RANK, HEAD_PAD - MLA_NOPE_DIM), F32))],
        axis=2).astype(BF16)
    wuvt_all = jnp.swapaxes(w_uv, 1, 2).astype(BF16)
    wb_all, wo_all = w_branch.astype(BF16), w_out.astype(BF16)
    wup_all, wdown_all = w_up.astype(BF16), w_down.astype(BF16)

    rows = lambda g: g.reshape(depth, 1, -1)
    g_cq, g_ckv, bf_all = rows(g_cq), rows(g_ckv), rows(bf_all)
    g_pre_mix, g_post_mix, g_pre_mlp, g_post_mlp = (rows(g) for g in (g_pre_mix, g_post_mix, g_pre_mlp,
                                                                       g_post_mlp))

    x2 = x.reshape(m, d)
    for l in range(depth):
        weights = [(w, l) for w in (wk_all, wt_all, wsmall_all, wf_all, g_cq, g_ckv, wuqt_all, wuk_all,
                                    wuvt_all)]
        kpad, qvt, lqt, fpre = _inproj(x2, (g_pre_mix, l), weights, rope_p, rope_mk, rope_mq)
        kpad3 = kpad.reshape(b, s, 4 * QK_PAD)
        fox_qb, fox_kb = _fox_bias(fpre.reshape(b, s, HEAD_PAD), (bf_all, l), fox_consts)
        moba_qb = _moba_bias(qvt, 5, kpad3, 2, moba_ind, moba_hmask)
        y_fox = _attention(qvt, 3, HEAD_DIM, kpad3, 0, qvt, 0, causal_bias, q_bias=fox_qb, k_bias=fox_kb)
        y_dil = _attention(qvt, 4, HEAD_DIM, kpad3, 1, qvt, 1, dil_bias, dilated=True)
        y_moba = _attention(qvt, 5, HEAD_DIM, kpad3, 2, qvt, 2, causal_bias, q_bias=moba_qb, k_bias=moba_kb)
        y_mla = _attention(lqt, 0, MLA_QK_DIM, kpad3, 3, qvt, 6, causal_bias)
        ys = [y.reshape(m, BRANCH_WIDTH) for y in (y_fox, y_dil, y_moba, y_mla)]
        x2 = _merge(x2, (g_pre_mix, l), (wgate_all, l), ys, (wb_all, l), (wo_all, l), (g_post_mix, l))
        x2 = _mlp(x2, (g_pre_mlp, l), (wup_all, l), (wdown_all, l), (g_post_mlp, l))
    return x2.reshape(b, s, d)
```

```python
import functools

import numpy as np
import jax
import jax.numpy as jnp
from jax import lax
from jax.experimental import pallas as pl
from jax.experimental.pallas import tpu as pltpu

LANES = 128
F32_SUBLANES = 8
BF16_SUBLANES = 16
V7X_VMEM_BYTES = 64 * 1024 * 1024

D_MODEL = 1024
SEQ = 2048
HEAD_DIM = 64
N_HEADS = 4
BRANCH_WIDTH = N_HEADS * HEAD_DIM
HEAD_PAD = LANES
QK_PAD = N_HEADS * HEAD_PAD
BIAS_SLOTS = BF16_SUBLANES
ROT_DIM = HEAD_DIM // 4
ROPE_THETA = 500000.0
DILATED_PAIRS = ((128, 1), (512, 4), (2048, 16))
MOBA_BLOCK = 256
MOBA_TOPK = 3
N_MOBA_BLOCKS = SEQ // MOBA_BLOCK
MLA_Q_RANK = 256
MLA_KV_RANK = 128
MLA_NOPE_DIM = 64
MLA_ROPE_DIM = 32
MLA_QK_DIM = MLA_NOPE_DIM + MLA_ROPE_DIM
D_FF = 4 * D_MODEL
NORM_EPS = 1e-6
NEG = -1e30
LOG2E = 1.4426950408889634

TOKEN_TILE = 512
ATTN_TILE = 256
N_ATTN_TILES = SEQ // ATTN_TILE
ACC_ROWS = HEAD_DIM + BF16_SUBLANES
DIAG_STEPS_PER_TRIP = N_ATTN_TILES
BELOW_STEPS_PER_TRIP = N_ATTN_TILES * (N_ATTN_TILES - 1) // 4
FF_CHUNK = 512
MERGE_CHUNK = 256
VMEM_LIMIT = V7X_VMEM_BYTES * 3 // 4

F32 = jnp.float32
BF16 = jnp.bfloat16


def _dot(a, b):
    return jnp.dot(a, b, preferred_element_type=F32)


def _dot_nt(a, b):
    return lax.dot_general(a, b, (((1,), (1,)), ((), ())), preferred_element_type=F32)


def _rms(x, g):
    return x * lax.rsqrt(jnp.mean(x * x, axis=-1, keepdims=True) + NORM_EPS) * g


def _rope(y, a, bm, bp, half, axis):
    n = y.shape[axis]
    return y * a + pltpu.roll(y, n - half, axis) * bm + pltpu.roll(y, half, axis) * bp


def _split3(x):
    hi = x.astype(BF16)
    r = x - hi.astype(F32)
    mid = r.astype(BF16)
    lo = (r - mid.astype(F32)).astype(BF16)
    return hi, mid, lo


def _rope_tables():
    def tables(dim):
        inv_freq = 1.0 / (ROPE_THETA ** (jnp.arange(0, dim, 2, dtype=F32) / dim))
        ang = jnp.arange(SEQ, dtype=F32)[:, None] * inv_freq[None, :]
        return jnp.cos(ang), jnp.sin(ang)

    def expand(cos, sin, width, period, start, half):
        pos = np.arange(width) % period - start
        first = (pos >= 0) & (pos < half)
        second = (pos >= half) & (pos < 2 * half)
        idx = np.where(first, pos, np.where(second, pos - half, 0))
        a = jnp.where((first | second)[None, :], cos[:, idx], 1.0)
        bm = jnp.where(first[None, :], -sin[:, idx], 0.0)
        bp = jnp.where(second[None, :], sin[:, idx], 0.0)
        return a, bm, bp

    cos_p, sin_p = tables(ROT_DIM)
    cos_m, sin_m = tables(MLA_ROPE_DIM)
    partial = expand(cos_p, sin_p, BRANCH_WIDTH, HEAD_DIM, 0, ROT_DIM // 2)
    mla_k = expand(cos_m, sin_m, HEAD_PAD, HEAD_PAD, MLA_NOPE_DIM, MLA_ROPE_DIM // 2)
    mla_q = expand(cos_m, sin_m, MLA_QK_DIM, MLA_QK_DIM, MLA_NOPE_DIM, MLA_ROPE_DIM // 2)
    return partial, mla_k, mla_q


def _fox_constants():
    to_k = np.zeros((HEAD_PAD, HEAD_PAD), np.float32)
    to_q = np.zeros((N_HEADS * BIAS_SLOTS, HEAD_PAD), np.float32)
    q_ones = np.zeros((N_HEADS * BIAS_SLOTS, 1), np.float32)
    k_ones = np.zeros((1, HEAD_PAD), np.float32)
    for h in range(N_HEADS):
        for j in range(3):
            to_k[4 * j + h, HEAD_DIM + BIAS_SLOTS * h + 3 + j] = 1.0
            to_q[BIAS_SLOTS * h + j, 4 * j + h] = 1.0
        q_ones[BIAS_SLOTS * h + 3:BIAS_SLOTS * h + 6] = 1.0
        k_ones[0, HEAD_DIM + BIAS_SLOTS * h:HEAD_DIM + BIAS_SLOTS * h + 3] = 1.0
    return [jnp.asarray(to_k, BF16), jnp.asarray(to_q, BF16), jnp.asarray(q_ones), jnp.asarray(k_ones)]


def _moba_key_bias():
    out = np.zeros((1, SEQ, HEAD_PAD), np.float32)
    blk = np.arange(SEQ) // MOBA_BLOCK
    for h in range(N_HEADS):
        for n in range(N_MOBA_BLOCKS):
            out[0, :, HEAD_DIM + BIAS_SLOTS * h + n] = blk == n
    return jnp.asarray(out, BF16)


def _block_indicator():
    ind = np.zeros((HEAD_PAD, SEQ), np.float32)
    for h in range(N_HEADS):
        for n in range(N_MOBA_BLOCKS):
            ind[N_MOBA_BLOCKS * h + n, n * MOBA_BLOCK:(n + 1) * MOBA_BLOCK] = 1.0
    return jnp.asarray(ind, BF16)


def _block_head_mask():
    m = np.zeros((HEAD_PAD, QK_PAD), np.float32)
    for h in range(N_HEADS):
        m[N_MOBA_BLOCKS * h:N_MOBA_BLOCKS * (h + 1), h * HEAD_PAD:h * HEAD_PAD + HEAD_DIM] = 1.0
    return jnp.asarray(m)


def _causal_bias():
    key = np.arange(ATTN_TILE)[:, None]
    query = np.arange(ATTN_TILE)[None, :]
    return jnp.asarray(np.where(key <= query, 0.0, NEG).astype(np.float32)[None])


def _dilated_bias():
    r = np.arange(ATTN_TILE)[None, :]
    c = np.arange(ATTN_TILE)[:, None]
    count = np.zeros((SEQ // ATTN_TILE, ATTN_TILE, ATTN_TILE), np.float32)
    for delta in range(SEQ // ATTN_TILE):
        d = delta * ATTN_TILE + r - c
        for window, dil in DILATED_PAIRS:
            count[delta] += (d >= 0) & (d <= window) & (d % dil == 0)
    return jnp.asarray(np.where(count > 0, np.log2(np.maximum(count, 1.0)), NEG).astype(np.float32))


def _full(shape):
    return pl.BlockSpec(shape, lambda *_: (0,) * len(shape), pipeline_mode=pl.Buffered(1))


def _whole(operands):
    arrays, specs = [], []
    for op in operands:
        if isinstance(op, tuple):
            stacked, layer = op
            index = (layer,) + (0,) * (stacked.ndim - 1)
            arrays.append(stacked)
            specs.append(pl.BlockSpec((None,) + stacked.shape[1:], lambda *_, index=index: index,
                                      pipeline_mode=pl.Buffered(1)))
        else:
            arrays.append(op)
            specs.append(_full(op.shape))
    return arrays, specs


def _params(*sem):
    return pltpu.CompilerParams(dimension_semantics=sem, vmem_limit_bytes=VMEM_LIMIT)


def _inproj_kernel(x_ref, g_ref, wk_ref, wt_ref, wsmall_ref, wf_ref, gq_ref, gkv_ref, wuqt_ref, wuk_ref,
                   wuvt_ref, ra_ref, rbm_ref, rbp_ref, rat_ref, rbmt_ref, rbpt_ref, mra_ref, mrbm_ref,
                   mrbp_ref, mrat_ref, mrbmt_ref, mrbpt_ref, kpad_ref, qvt_ref, lqt_ref, f_ref):
    hf = _rms(x_ref[...], g_ref[...])
    h = hf.astype(BF16)
    tm = h.shape[0]

    lane = lax.broadcasted_iota(jnp.int32, f_ref.shape, 1)
    f = jnp.zeros(f_ref.shape, F32)
    for j in range(N_HEADS):
        col = jnp.sum(hf * wf_ref[j:j + 1, :], axis=-1, keepdims=True)
        f = jnp.where(lane == j, col, f)
    f_ref[...] = f

    small = _dot(h, wsmall_ref[...])
    half = MLA_ROPE_DIM // 2
    cq = _rms(small[:, 0:MLA_Q_RANK], gq_ref[...]).astype(BF16)
    ckv = _rms(small[:, MLA_Q_RANK:MLA_Q_RANK + MLA_KV_RANK], gkv_ref[...]).astype(BF16)
    kr = _rope(small[:, MLA_Q_RANK + MLA_KV_RANK:], mra_ref[...], mrbm_ref[...], mrbp_ref[...], half, 1)
    qf_t = _dot_nt(wuqt_ref[...], cq)
    kf = _dot(ckv, wuk_ref[...])
    for hd in range(N_HEADS):
        rows = slice(hd * MLA_QK_DIM, (hd + 1) * MLA_QK_DIM)
        q_h = _rope(qf_t[rows, :], mrat_ref[...], mrbmt_ref[...], mrbpt_ref[...], half, 0)
        lqt_ref[0, rows, :] = (q_h * (MLA_QK_DIM ** -0.5 * LOG2E)).astype(BF16)
        col = 3 * QK_PAD + hd * HEAD_PAD
        kpad_ref[:, col:col + HEAD_PAD] = (kf[:, hd * HEAD_PAD:(hd + 1) * HEAD_PAD] + kr).astype(BF16)
    qvt_ref[0, 6 * BRANCH_WIDTH:, :] = _dot_nt(wuvt_ref[...], ckv).astype(BF16)

    low_half = lax.broadcasted_iota(jnp.int32, (tm, HEAD_PAD), 1) < HEAD_DIM
    for c in range(3):
        y = _dot(h, wk_ref[:, c * BRANCH_WIDTH:(c + 1) * BRANCH_WIDTH])
        if c >= 1:
            y = _rope(y, ra_ref[...], rbm_ref[...], rbp_ref[...], ROT_DIM // 2, 1)
        for pair in range(2):
            both = y[:, pair * HEAD_PAD:(pair + 1) * HEAD_PAD]
            for odd, src in enumerate((both, pltpu.roll(both, HEAD_DIM, 1))):
                col = c * QK_PAD + (2 * pair + odd) * HEAD_PAD
                kpad_ref[:, col:col + HEAD_PAD] = jnp.where(low_half, src, 0.0).astype(BF16)

    t_all = _dot_nt(wt_ref[...], h)
    for c in range(6):
        rows = slice(c * BRANCH_WIDTH, (c + 1) * BRANCH_WIDTH)
        y = t_all[rows, :]
        if c >= 4:
            y = _rope(y, rat_ref[...], rbmt_ref[...], rbpt_ref[...], ROT_DIM // 2, 0)
        if c >= 3:
            y = y * (HEAD_DIM ** -0.5 * LOG2E)
        qvt_ref[0, rows, :] = y.astype(BF16)


def _inproj(x2, g, weights, rope_p, rope_mk, rope_mq):
    m = x2.shape[0]
    tm = TOKEN_TILE
    seq_tiles = SEQ // tm
    row = lambda i: (i, 0)
    by_batch = lambda i: (i // seq_tiles, 0, i % seq_tiles)
    seq_rows = lambda width: pl.BlockSpec((tm, width), lambda i: (i % seq_tiles, 0))
    seq_cols = lambda height: pl.BlockSpec((height, tm), lambda i: (0, i % seq_tiles))
    tables = ([seq_rows(BRANCH_WIDTH)] * 3 + [seq_cols(BRANCH_WIDTH)] * 3
              + [seq_rows(HEAD_PAD)] * 3 + [seq_cols(MLA_QK_DIM)] * 3)
    (g, *weights), whole_specs = _whole([g] + list(weights))
    return pl.pallas_call(
        _inproj_kernel,
        grid=(m // tm,),
        in_specs=[pl.BlockSpec((tm, D_MODEL), row)] + whole_specs + tables,
        out_specs=[pl.BlockSpec((tm, 4 * QK_PAD), row), pl.BlockSpec((1, 7 * BRANCH_WIDTH, tm), by_batch),
                   pl.BlockSpec((1, N_HEADS * MLA_QK_DIM, tm), by_batch), pl.BlockSpec((tm, HEAD_PAD), row)],
        out_shape=[jax.ShapeDtypeStruct((m, 4 * QK_PAD), BF16),
                   jax.ShapeDtypeStruct((m // SEQ, 7 * BRANCH_WIDTH, SEQ), BF16),
                   jax.ShapeDtypeStruct((m // SEQ, N_HEADS * MLA_QK_DIM, SEQ), BF16),
                   jax.ShapeDtypeStruct((m, HEAD_PAD), F32)],
        compiler_params=_params("parallel"),
        name="inproj",
    )(x2, g, *weights, *rope_p, *[t.T for t in rope_p], *rope_mk, *[t.T for t in rope_mq])


def _fox_bias_kernel(f_ref, bf_ref, to_k_ref, to_q_ref, q_ones_ref, k_ones_ref, qb_ref, kb_ref):
    chunk = 256
    ri = lax.broadcasted_iota(jnp.int32, (chunk, chunk), 0)
    ci = lax.broadcasted_iota(jnp.int32, (chunk, chunk), 1)
    tril = jnp.where(ri >= ci, 1.0, 0.0).astype(BF16)
    lane = lax.broadcasted_iota(jnp.int32, (chunk, HEAD_PAD), 1)
    carry = jnp.zeros((1, HEAD_PAD), F32)
    for c in range(SEQ // chunk):
        rows = slice(c * chunk, (c + 1) * chunk)
        z = f_ref[0, rows, :] + bf_ref[...]
        logf = jnp.minimum(z, 0.0) - jnp.log(1.0 + jnp.exp(-jnp.abs(z)))
        hi, mid, lo = _split3(logf)
        cum = _dot(tril, hi) + _dot(tril, mid) + _dot(tril, lo) + carry
        carry = cum[chunk - 1:chunk, :]
        c_hi, c_mid, c_lo = (p.astype(F32) for p in _split3(cum * LOG2E))
        packed = jnp.where(lane < N_HEADS, c_hi,
                           jnp.where(lane < 2 * N_HEADS, pltpu.roll(c_mid, N_HEADS, 1),
                                     pltpu.roll(c_lo, 2 * N_HEADS, 1))).astype(BF16)
        kb_ref[0, rows, :] = (k_ones_ref[...] - _dot(packed, to_k_ref[...])).astype(BF16)
        qb_ref[0, :, rows] = (_dot_nt(to_q_ref[...], packed) + q_ones_ref[...]).astype(BF16)


def _fox_bias(fpre3, bf, consts):
    b = fpre3.shape[0]
    (bf, *consts), whole_specs = _whole([bf] + list(consts))
    return pl.pallas_call(
        _fox_bias_kernel,
        grid=(b,),
        in_specs=[pl.BlockSpec((1, SEQ, HEAD_PAD), lambda i: (i, 0, 0))] + whole_specs,
        out_specs=[pl.BlockSpec((1, N_HEADS * BIAS_SLOTS, SEQ), lambda i: (i, 0, 0)),
                   pl.BlockSpec((1, SEQ, HEAD_PAD), lambda i: (i, 0, 0))],
        out_shape=[jax.ShapeDtypeStruct((b, N_HEADS * BIAS_SLOTS, SEQ), BF16),
                   jax.ShapeDtypeStruct((b, SEQ, HEAD_PAD), BF16)],
        compiler_params=_params("parallel"),
        name="fox_bias",
    )(fpre3, bf, *consts)


def _moba_bias_kernel(qt_ref, k_ref, ind_ref, hmask_ref, qb_ref):
    kmean = _dot(ind_ref[...], k_ref[0]) * (1.0 / MOBA_BLOCK) * hmask_ref[...]
    pieces = _split3(kmean)
    zeros = jnp.zeros((HEAD_PAD - HEAD_DIM, SEQ), BF16)
    qt_pad = jnp.concatenate(
        [part for h in range(N_HEADS) for part in (qt_ref[0, h * HEAD_DIM:(h + 1) * HEAD_DIM, :], zeros)], axis=0)
    gate = _dot(pieces[0], qt_pad) + _dot(pieces[1], qt_pad) + _dot(pieces[2], qt_pad)
    blk = lax.broadcasted_iota(jnp.int32, (N_MOBA_BLOCKS, SEQ), 0)
    own = lax.broadcasted_iota(jnp.int32, (N_MOBA_BLOCKS, SEQ), 1) // MOBA_BLOCK
    for h in range(N_HEADS):
        g = gate[h * N_MOBA_BLOCKS:(h + 1) * N_MOBA_BLOCKS, :]
        rank = jnp.zeros(g.shape, F32)
        for r in range(1, N_MOBA_BLOCKS):
            other = pltpu.roll(g, N_MOBA_BLOCKS - r, 0)
            other_blk = jnp.where(blk + r >= N_MOBA_BLOCKS, blk + r - N_MOBA_BLOCKS, blk + r)
            ahead = (other > g) | ((other == g) & (other_blk < blk))
            rank = rank + jnp.where(ahead & (other_blk < own), 1.0, 0.0)
        keep = ((blk < own) & (rank < MOBA_TOPK)) | (blk == own)
        bias = jnp.concatenate([jnp.where(keep, 0.0, NEG), jnp.zeros(g.shape, F32)], axis=0)
        qb_ref[0, h * BIAS_SLOTS:(h + 1) * BIAS_SLOTS, :] = bias.astype(BF16)


def _moba_bias(qvt, q_row, kpad3, k_col, ind, hmask):
    b = qvt.shape[0]
    return pl.pallas_call(
        _moba_bias_kernel,
        grid=(b,),
        in_specs=[pl.BlockSpec((1, BRANCH_WIDTH, SEQ), lambda i: (i, q_row, 0)),
                  pl.BlockSpec((1, SEQ, QK_PAD), lambda i: (i, 0, k_col)),
                  _full(ind.shape), _full(hmask.shape)],
        out_specs=pl.BlockSpec((1, N_HEADS * BIAS_SLOTS, SEQ), lambda i: (i, 0, 0)),
        out_shape=jax.ShapeDtypeStruct((b, N_HEADS * BIAS_SLOTS, SEQ), BF16),
        compiler_params=_params("parallel"),
        name="moba_bias",
    )(qvt, kpad3, ind, hmask)


def _attn_kernel(*refs, q_dim, biased, dilated):
    if biased:
        qt_ref, qb_ref, k_ref, kb_ref, vt_ref, tab_ref, o_ref, sa_ref, sb_ref, m_ref, acc_ref = refs
    else:
        qt_ref, k_ref, vt_ref, tab_ref, o_ref, sa_ref, sb_ref, m_ref, acc_ref = refs
    t = ATTN_TILE
    n_tiles = N_ATTN_TILES
    ones_rows = jnp.ones((ACC_ROWS - HEAD_DIM, t), BF16)

    m_ref[...] = jnp.full(m_ref.shape, NEG, F32)
    acc_ref[...] = jnp.zeros(acc_ref.shape, F32)

    def scores(i, j, s_ref, bias):
        qo = pl.multiple_of(i * t, t)
        ko = pl.multiple_of(j * t, t)
        for h in range(N_HEADS):
            parts = [qt_ref[0, h * q_dim:(h + 1) * q_dim, pl.ds(qo, t)]]
            k_op = k_ref[0, pl.ds(ko, t), h * HEAD_PAD:(h + 1) * HEAD_PAD]
            if biased:
                k_op = k_op + kb_ref[0, pl.ds(ko, t), :]
                if h > 0:
                    parts.append(jnp.zeros((h * BIAS_SLOTS, t), BF16))
                parts.append(qb_ref[0, h * BIAS_SLOTS:(h + 1) * BIAS_SLOTS, pl.ds(qo, t)])
            fill = HEAD_PAD - sum(p.shape[0] for p in parts)
            if fill:
                parts.append(jnp.zeros((fill, t), BF16))
            s = _dot(k_op, jnp.concatenate(parts, axis=0))
            s_ref[h] = (s if bias is None else s + bias).astype(BF16)

    def update(i, j, s_ref):
        qo = pl.multiple_of(i * t, t)
        ko = pl.multiple_of(j * t, t)
        for h in range(N_HEADS):
            s = s_ref[h]
            m_old = m_ref[h, :, pl.ds(qo, t)]
            m_new = jnp.maximum(m_old, jnp.max(s, axis=0, keepdims=True).astype(F32))
            m_ref[h, :, pl.ds(qo, t)] = m_new
            p = jnp.exp2(s - m_new.astype(BF16))
            v_aug = jnp.concatenate([vt_ref[0, h * HEAD_DIM:(h + 1) * HEAD_DIM, pl.ds(ko, t)], ones_rows], axis=0)
            rows = slice(h * ACC_ROWS, (h + 1) * ACC_ROWS)
            acc_ref[rows, pl.ds(qo, t)] = (acc_ref[rows, pl.ds(qo, t)] * jnp.exp2(m_old - m_new)
                                           + _dot(v_aug, p))

    def pipeline(start, succ, n_steps, per_trip, bias_of):
        def trip(_, ij):
            cur = ij
            for step in range(per_trip):
                s_cur, s_next = (sa_ref, sb_ref) if step % 2 == 0 else (sb_ref, sa_ref)
                nxt = succ(*cur)
                scores(*nxt, s_next, bias_of(*nxt))
                update(*cur, s_cur)
                cur = nxt
            return cur

        assert n_steps % per_trip == 0 and per_trip % 2 == 0
        scores(*start, sa_ref, bias_of(*start))
        lax.fori_loop(0, n_steps // per_trip, trip, (jnp.int32(start[0]), jnp.int32(start[1])))

    def succ_diag(i, j):
        nxt = jnp.minimum(i + 1, n_tiles - 1)
        return nxt, nxt

    def succ_below(i, j):
        last = j == i - 1
        return jnp.where(last, jnp.minimum(i + 1, n_tiles - 1), i), jnp.where(last, 0, j + 1)

    pipeline((0, 0), succ_diag, n_tiles, DIAG_STEPS_PER_TRIP, lambda i, j: tab_ref[0])
    pipeline((1, 0), succ_below, n_tiles * (n_tiles - 1) // 2, BELOW_STEPS_PER_TRIP,
             (lambda i, j: tab_ref[i - j]) if dilated else (lambda i, j: None))

    for i in range(n_tiles):
        q_cols = slice(i * t, (i + 1) * t)
        heads = []
        for h in range(N_HEADS):
            denom = acc_ref[h * ACC_ROWS + HEAD_DIM:h * ACC_ROWS + HEAD_DIM + 1, q_cols]
            heads.append(acc_ref[h * ACC_ROWS:h * ACC_ROWS + HEAD_DIM, q_cols] * (1.0 / denom))
        o_ref[0, q_cols, :] = jnp.concatenate(heads, axis=0).T.astype(o_ref.dtype)


def _attention(qt, q_row, q_dim, k, k_col, vt, v_row, table, dilated=False, q_bias=None, k_bias=None):
    b = qt.shape[0]
    t = ATTN_TILE
    biased = q_bias is not None
    in_specs = [pl.BlockSpec((1, N_HEADS * q_dim, SEQ), lambda bi: (bi, q_row, 0))]
    args = [qt]
    if biased:
        in_specs.append(pl.BlockSpec((1, N_HEADS * BIAS_SLOTS, SEQ), lambda bi: (bi, 0, 0)))
        args.append(q_bias)
    in_specs.append(pl.BlockSpec((1, SEQ, QK_PAD), lambda bi: (bi, 0, k_col)))
    args.append(k)
    if biased:
        per_batch = k_bias.shape[0] > 1
        in_specs.append(pl.BlockSpec((1, SEQ, HEAD_PAD), lambda bi: (bi if per_batch else 0, 0, 0)))
        args.append(k_bias)
    in_specs += [pl.BlockSpec((1, BRANCH_WIDTH, SEQ), lambda bi: (bi, v_row, 0)), _full(table.shape)]
    args += [vt, table]
    return pl.pallas_call(
        functools.partial(_attn_kernel, q_dim=q_dim, biased=biased, dilated=dilated),
        grid=(b,),
        in_specs=in_specs,
        out_specs=pl.BlockSpec((1, SEQ, BRANCH_WIDTH), lambda bi: (bi, 0, 0)),
        out_shape=jax.ShapeDtypeStruct((b, SEQ, BRANCH_WIDTH), BF16),
        scratch_shapes=[pltpu.VMEM((N_HEADS, t, t), BF16), pltpu.VMEM((N_HEADS, t, t), BF16),
                        pltpu.VMEM((N_HEADS, 1, SEQ), F32), pltpu.VMEM((N_HEADS * ACC_ROWS, SEQ), F32)],
        compiler_params=_params("parallel"),
        name="attn_dil" if dilated else ("attn_bias" if biased else "attn"),
    )(*args)


def _merge_kernel(x_ref, gpre_ref, wg_ref, y0_ref, y1_ref, y2_ref, y3_ref, wb_ref, wo_ref, gpost_ref,
                  o_ref):
    x = x_ref[...]
    h = _rms(x, gpre_ref[...]).astype(BF16)
    mix = None
    for c in range(D_MODEL // MERGE_CHUNK):
        cols = slice(c * MERGE_CHUNK, (c + 1) * MERGE_CHUNK)
        merged = None
        for n, y_ref in enumerate((y0_ref, y1_ref, y2_ref, y3_ref)):
            gate = jax.nn.sigmoid(_dot(h, wg_ref[:, n * D_MODEL + c * MERGE_CHUNK:
                                                 n * D_MODEL + (c + 1) * MERGE_CHUNK]))
            term = gate * _dot(y_ref[...], wb_ref[n, :, cols])
            merged = term if merged is None else merged + term
        part = _dot(merged.astype(BF16), wo_ref[cols, :])
        mix = part if mix is None else mix + part
    o_ref[...] = x + _rms(mix, gpost_ref[...])


def _merge(x2, gpre, wg, ys, wb, wo, gpost):
    m = x2.shape[0]
    tm = TOKEN_TILE
    row = lambda i: (i, 0)
    yspec = pl.BlockSpec((tm, BRANCH_WIDTH), row)
    (gpre, wg, wb, wo, gpost), (s_gpre, s_wg, s_wb, s_wo, s_gpost) = _whole([gpre, wg, wb, wo, gpost])
    return pl.pallas_call(
        _merge_kernel,
        grid=(m // tm,),
        in_specs=[pl.BlockSpec((tm, D_MODEL), row), s_gpre, s_wg, yspec, yspec, yspec, yspec,
                  s_wb, s_wo, s_gpost],
        out_specs=pl.BlockSpec((tm, D_MODEL), row),
        out_shape=jax.ShapeDtypeStruct((m, D_MODEL), F32),
        compiler_params=_params("parallel"),
        name="merge",
    )(x2, gpre, wg, *ys, wb, wo, gpost)


def _mlp_kernel(x_ref, gpre_ref, wup_ref, wdown_ref, gpost_ref, o_ref):
    x = x_ref[...]
    h = _rms(x, gpre_ref[...]).astype(BF16)
    acc = None
    for c in range(D_FF // FF_CHUNK):
        cols = slice(c * FF_CHUNK, (c + 1) * FF_CHUNK)
        up = jnp.maximum(_dot(h, wup_ref[:, cols]), 0.0)
        term = _dot((up * up).astype(BF16), wdown_ref[cols, :])
        acc = term if acc is None else acc + term
    o_ref[...] = x + _rms(acc, gpost_ref[...])


def _mlp(x2, gpre, wup, wdown, gpost):
    m = x2.shape[0]
    tm = TOKEN_TILE
    row = lambda i: (i, 0)
    (gpre, wup, wdown, gpost), whole_specs = _whole([gpre, wup, wdown, gpost])
    return pl.pallas_call(
        _mlp_kernel,
        grid=(m // tm,),
        in_specs=[pl.BlockSpec((tm, D_MODEL), row)] + whole_specs,
        out_specs=pl.BlockSpec((tm, D_MODEL), row),
        out_shape=jax.ShapeDtypeStruct((m, D_MODEL), F32),
        compiler_params=_params("parallel"),
        name="mlp",
    )(x2, gpre, wup, wdown, gpost)


def kernel(x, w_in, b_forget, g_cq, g_ckv, w_uq, w_uk, w_uv, w_branch, w_out, w_up, w_down,
           g_pre_mix, g_post_mix, g_pre_mlp, g_post_mlp):
    b, s, d = x.shape
    assert (s, d) == (SEQ, D_MODEL)
    depth = w_in.shape[0]
    m = b * s
    rope_p, rope_mk, rope_mq = _rope_tables()
    fox_consts = _fox_constants()
    moba_ind, moba_hmask, moba_kb = _block_indicator(), _block_head_mask(), _moba_key_bias()
    dil_bias = _dilated_bias()
    causal_bias = _causal_bias()

    bw = BRANCH_WIDTH
    o_f = 3 * bw
    o_dil = o_f + N_HEADS
    o_moba = o_dil + 3 * bw
    o_cq = o_moba + 3 * bw
    o_ckv = o_cq + MLA_Q_RANK
    o_kr = o_ckv + MLA_KV_RANK
    o_gate = o_kr + MLA_ROPE_DIM

    qkv_at = (0, o_dil, o_moba)
    wk_all = jnp.concatenate([w_in[:, :, o + bw:o + 2 * bw] for o in qkv_at], axis=2).astype(BF16)
    wt_all = jnp.swapaxes(jnp.concatenate([w_in[:, :, o + 2 * bw:o + 3 * bw] for o in qkv_at]
                                          + [w_in[:, :, o:o + bw] for o in qkv_at], axis=2), 1, 2).astype(BF16)
    wsmall_all = jnp.concatenate(
        [w_in[:, :, o_cq:o_kr], jnp.zeros((depth, d, MLA_NOPE_DIM), F32), w_in[:, :, o_kr:o_gate],
         jnp.zeros((depth, d, HEAD_PAD - MLA_QK_DIM), F32)], axis=2).astype(BF16)
    wf_all = jnp.concatenate([jnp.swapaxes(w_in[:, :, o_f:o_dil], 1, 2),
                              jnp.zeros((depth, F32_SUBLANES - N_HEADS, d), F32)], axis=1)
    wgate_all = w_in[:, :, o_gate:].astype(BF16)
    bf_all = jnp.concatenate([b_forget, jnp.zeros((depth, HEAD_PAD - N_HEADS), F32)], axis=1)
    wuqt_all = jnp.swapaxes(w_uq, 1, 2).astype(BF16)
    wuk_all = jnp.concatenate(
        [part for h in range(N_HEADS) for part in (w_uk[:, :, h * MLA_NOPE_DIM:(h + 1) * MLA_NOPE_DIM],
                                                   jnp.zeros((depth, MLA_KV_RANK, HEAD_PAD - MLA_NOPE_DIM), F32))],
        axis=2).astype(BF16)
    wuvt_all = jnp.swapaxes(w_uv, 1, 2).astype(BF16)
    wb_all, wo_all = w_branch.astype(BF16), w_out.astype(BF16)
    wup_all, wdown_all = w_up.astype(BF16), w_down.astype(BF16)

    rows = lambda g: g.reshape(depth, 1, -1)
    g_cq, g_ckv, bf_all = rows(g_cq), rows(g_ckv), rows(bf_all)
    g_pre_mix, g_post_mix, g_pre_mlp, g_post_mlp = (rows(g) for g in (g_pre_mix, g_post_mix, g_pre_mlp,
                                                                       g_post_mlp))

    x2 = x.reshape(m, d)
    for l in range(depth):
        weights = [(w, l) for w in (wk_all, wt_all, wsmall_all, wf_all, g_cq, g_ckv, wuqt_all, wuk_all,
                                    wuvt_all)]
        kpad, qvt, lqt, fpre = _inproj(x2, (g_pre_mix, l), weights, rope_p, rope_mk, rope_mq)
        kpad3 = kpad.reshape(b, s, 4 * QK_PAD)
        fox_qb, fox_kb = _fox_bias(fpre.reshape(b, s, HEAD_PAD), (bf_all, l), fox_consts)
        moba_qb = _moba_bias(qvt, 5, kpad3, 2, moba_ind, moba_hmask)
        y_fox = _attention(qvt, 3, HEAD_DIM, kpad3, 0, qvt, 0, causal_bias, q_bias=fox_qb, k_bias=fox_kb)
        y_dil = _attention(qvt, 4, HEAD_DIM, kpad3, 1, qvt, 1, dil_bias, dilated=True)
        y_moba = _attention(qvt, 5, HEAD_DIM, kpad3, 2, qvt, 2, causal_bias, q_bias=moba_qb, k_bias=moba_kb)
        y_mla = _attention(lqt, 0, MLA_QK_DIM, kpad3, 3, qvt, 6, causal_bias)
        ys = [y.reshape(m, BRANCH_WIDTH) for y in (y_fox, y_dil, y_moba, y_mla)]
        x2 = _merge(x2, (g_pre_mix, l), (wgate_all, l), ys, (wb_all, l), (wo_all, l), (g_post_mix, l))
        x2 = _mlp(x2, (g_pre_mlp, l), (wup_all, l), (wdown_all, l), (g_post_mlp, l))
    return x2.reshape(b, s, d)
```

```python
import functools

import numpy as np
import jax
import jax.numpy as jnp
from jax import lax
from jax.experimental import pallas as pl
from jax.experimental.pallas import tpu as pltpu

LANES = 128
F32_SUBLANES = 8
BF16_SUBLANES = 16
V7X_VMEM_BYTES = 64 * 1024 * 1024

D_MODEL = 1024
SEQ = 2048
HEAD_DIM = 64
N_HEADS = 4
BRANCH_WIDTH = N_HEADS * HEAD_DIM
HEAD_PAD = LANES
QK_PAD = N_HEADS * HEAD_PAD
BIAS_SLOTS = BF16_SUBLANES
ROT_DIM = HEAD_DIM // 4
ROPE_THETA = 500000.0
DILATED_PAIRS = ((128, 1), (512, 4), (2048, 16))
MOBA_BLOCK = 256
MOBA_TOPK = 3
N_MOBA_BLOCKS = SEQ // MOBA_BLOCK
MLA_Q_RANK = 256
MLA_KV_RANK = 128
MLA_NOPE_DIM = 64
MLA_ROPE_DIM = 32
MLA_QK_DIM = MLA_NOPE_DIM + MLA_ROPE_DIM
D_FF = 4 * D_MODEL
NORM_EPS = 1e-6
NEG = -1e30
LOG2E = 1.4426950408889634

TOKEN_TILE = 512
ATTN_TILE = 256
N_ATTN_TILES = SEQ // ATTN_TILE
ACC_ROWS = HEAD_DIM + BF16_SUBLANES
DIAG_STEPS_PER_TRIP = N_ATTN_TILES
BELOW_STEPS_PER_TRIP = N_ATTN_TILES * (N_ATTN_TILES - 1) // 4
FF_CHUNK = 512
MERGE_CHUNK = 256
VMEM_LIMIT = V7X_VMEM_BYTES * 3 // 4

F32 = jnp.float32
BF16 = jnp.bfloat16


def _dot(a, b):
    return jnp.dot(a, b, preferred_element_type=F32)


def _dot_nt(a, b):
    return lax.dot_general(a, b, (((1,), (1,)), ((), ())), preferred_element_type=F32)


def _rms(x, g):
    return x * lax.rsqrt(jnp.mean(x * x, axis=-1, keepdims=True) + NORM_EPS) * g


def _rope(y, a, bm, bp, half, axis):
    n = y.shape[axis]
    return y * a + pltpu.roll(y, n - half, axis) * bm + pltpu.roll(y, half, axis) * bp


def _split3(x):
    hi = x.astype(BF16)
    r = x - hi.astype(F32)
    mid = r.astype(BF16)
    lo = (r - mid.astype(F32)).astype(BF16)
    return hi, mid, lo


def _rope_tables():
    def tables(dim):
        inv_freq = 1.0 / (ROPE_THETA ** (jnp.arange(0, dim, 2, dtype=F32) / dim))
        ang = jnp.arange(SEQ, dtype=F32)[:, None] * inv_freq[None, :]
        return jnp.cos(ang), jnp.sin(ang)

    def expand(cos, sin, width, period, start, half):
        pos = np.arange(width) % period - start
        first = (pos >= 0) & (pos < half)
        second = (pos >= half) & (pos < 2 * half)
        idx = np.where(first, pos, np.where(second, pos - half, 0))
        a = jnp.where((first | second)[None, :], cos[:, idx], 1.0)
        bm = jnp.where(first[None, :], -sin[:, idx], 0.0)
        bp = jnp.where(second[None, :], sin[:, idx], 0.0)
        return a, bm, bp

    cos_p, sin_p = tables(ROT_DIM)
    cos_m, sin_m = tables(MLA_ROPE_DIM)
    partial = expand(cos_p, sin_p, BRANCH_WIDTH, HEAD_DIM, 0, ROT_DIM // 2)
    mla_k = expand(cos_m, sin_m, HEAD_PAD, HEAD_PAD, MLA_NOPE_DIM, MLA_ROPE_DIM // 2)
    mla_q = expand(cos_m, sin_m, MLA_QK_DIM, MLA_QK_DIM, MLA_NOPE_DIM, MLA_ROPE_DIM // 2)
    return partial, mla_k, mla_q


def _fox_constants():
    to_k = np.zeros((HEAD_PAD, HEAD_PAD), np.float32)
    to_q = np.zeros((N_HEADS * BIAS_SLOTS, HEAD_PAD), np.float32)
    q_ones = np.zeros((N_HEADS * BIAS_SLOTS, 1), np.float32)
    k_ones = np.zeros((1, HEAD_PAD), np.float32)
    for h in range(N_HEADS):
        for j in range(3):
            to_k[4 * j + h, HEAD_DIM + BIAS_SLOTS * h + 3 + j] = 1.0
            to_q[BIAS_SLOTS * h + j, 4 * j + h] = 1.0
        q_ones[BIAS_SLOTS * h + 3:BIAS_SLOTS * h + 6] = 1.0
        k_ones[0, HEAD_DIM + BIAS_SLOTS * h:HEAD_DIM + BIAS_SLOTS * h + 3] = 1.0
    return [jnp.asarray(to_k, BF16), jnp.asarray(to_q, BF16), jnp.asarray(q_ones), jnp.asarray(k_ones)]


def _moba_key_bias():
    out = np.zeros((1, SEQ, HEAD_PAD), np.float32)
    blk = np.arange(SEQ) // MOBA_BLOCK
    for h in range(N_HEADS):
        for n in range(N_MOBA_BLOCKS):
            out[0, :, HEAD_DIM + BIAS_SLOTS * h + n] = blk == n
    return jnp.asarray(out, BF16)


def _block_indicator():
    ind = np.zeros((HEAD_PAD, SEQ), np.float32)
    for h in range(N_HEADS):
        for n in range(N_MOBA_BLOCKS):
            ind[N_MOBA_BLOCKS * h + n, n * MOBA_BLOCK:(n + 1) * MOBA_BLOCK] = 1.0
    return jnp.asarray(ind, BF16)


def _block_head_mask():
    m = np.zeros((HEAD_PAD, QK_PAD), np.float32)
    for h in range(N_HEADS):
        m[N_MOBA_BLOCKS * h:N_MOBA_BLOCKS * (h + 1), h * HEAD_PAD:h * HEAD_PAD + HEAD_DIM] = 1.0
    return jnp.asarray(m)


def _causal_bias():
    key = np.arange(ATTN_TILE)[:, None]
    query = np.arange(ATTN_TILE)[None, :]
    return jnp.asarray(np.where(key <= query, 0.0, NEG).astype(np.float32)[None])


def _dilated_bias():
    r = np.arange(ATTN_TILE)[None, :]
    c = np.arange(ATTN_TILE)[:, None]
    count = np.zeros((SEQ // ATTN_TILE, ATTN_TILE, ATTN_TILE), np.float32)
    for delta in range(SEQ // ATTN_TILE):
        d = delta * ATTN_TILE + r - c
        for window, dil in DILATED_PAIRS:
            count[delta] += (d >= 0) & (d <= window) & (d % dil == 0)
    return jnp.asarray(np.where(count > 0, np.log2(np.maximum(count, 1.0)), NEG).astype(np.float32))


def _full(shape):
    return pl.BlockSpec(shape, lambda *_: (0,) * len(shape), pipeline_mode=pl.Buffered(1))


def _whole(operands):
    arrays, specs = [], []
    for op in operands:
        if isinstance(op, tuple):
            stacked, layer = op
            index = (layer,) + (0,) * (stacked.ndim - 1)
            arrays.append(stacked)
            specs.append(pl.BlockSpec((None,) + stacked.shape[1:], lambda *_, index=index: index,
                                      pipeline_mode=pl.Buffered(1)))
        else:
            arrays.append(op)
            specs.append(_full(op.shape))
    return arrays, specs


def _params(*sem):
    return pltpu.CompilerParams(dimension_semantics=sem, vmem_limit_bytes=VMEM_LIMIT)


def _inproj_kernel(x_ref, g_ref, wk_ref, wt_ref, wsmall_ref, wf_ref, gq_ref, gkv_ref, wuqt_ref, wuk_ref,
                   wuvt_ref, ra_ref, rbm_ref, rbp_ref, rat_ref, rbmt_ref, rbpt_ref, mra_ref, mrbm_ref,
                   mrbp_ref, mrat_ref, mrbmt_ref, mrbpt_ref, kpad_ref, qvt_ref, lqt_ref, f_ref):
    hf = _rms(x_ref[...], g_ref[...])
    h = hf.astype(BF16)
    tm = h.shape[0]

    lane = lax.broadcasted_iota(jnp.int32, f_ref.shape, 1)
    f = jnp.zeros(f_ref.shape, F32)
    for j in range(N_HEADS):
        col = jnp.sum(hf * wf_ref[j:j + 1, :], axis=-1, keepdims=True)
        f = jnp.where(lane == j, col, f)
    f_ref[...] = f

    small = _dot(h, wsmall_ref[...])
    half = MLA_ROPE_DIM // 2
    cq = _rms(small[:, 0:MLA_Q_RANK], gq_ref[...]).astype(BF16)
    ckv = _rms(small[:, MLA_Q_RANK:MLA_Q_RANK + MLA_KV_RANK], gkv_ref[...]).astype(BF16)
    kr = _rope(small[:, MLA_Q_RANK + MLA_KV_RANK:], mra_ref[...], mrbm_ref[...], mrbp_ref[...], half, 1)
    qf_t = _dot_nt(wuqt_ref[...], cq)
    kf = _dot(ckv, wuk_ref[...])
    for hd in range(N_HEADS):
        rows = slice(hd * MLA_QK_DIM, (hd + 1) * MLA_QK_DIM)
        q_h = _rope(qf_t[rows, :], mrat_ref[...], mrbmt_ref[...], mrbpt_ref[...], half, 0)
        lqt_ref[0, rows, :] = (q_h * (MLA_QK_DIM ** -0.5 * LOG2E)).astype(BF16)
        col = 3 * QK_PAD + hd * HEAD_PAD
        kpad_ref[:, col:col + HEAD_PAD] = (kf[:, hd * HEAD_PAD:(hd + 1) * HEAD_PAD] + kr).astype(BF16)
    qvt_ref[0, 6 * BRANCH_WIDTH:, :] = _dot_nt(wuvt_ref[...], ckv).astype(BF16)

    low_half = lax.broadcasted_iota(jnp.int32, (tm, HEAD_PAD), 1) < HEAD_DIM
    for c in range(3):
        y = _dot(h, wk_ref[:, c * BRANCH_WIDTH:(c + 1) * BRANCH_WIDTH])
        if c >= 1:
            y = _rope(y, ra_ref[...], rbm_ref[...], rbp_ref[...], ROT_DIM // 2, 1)
        for pair in range(2):
            both = y[:, pair * HEAD_PAD:(pair + 1) * HEAD_PAD]
            for odd, src in enumerate((both, pltpu.roll(both, HEAD_DIM, 1))):
                col = c * QK_PAD + (2 * pair + odd) * HEAD_PAD
                kpad_ref[:, col:col + HEAD_PAD] = jnp.where(low_half, src, 0.0).astype(BF16)

    t_all = _dot_nt(wt_ref[...], h)
    for c in range(6):
        rows = slice(c * BRANCH_WIDTH, (c + 1) * BRANCH_WIDTH)
        y = t_all[rows, :]
        if c >= 4:
            y = _rope(y, rat_ref[...], rbmt_ref[...], rbpt_ref[...], ROT_DIM // 2, 0)
        if c >= 3:
            y = y * (HEAD_DIM ** -0.5 * LOG2E)
        qvt_ref[0, rows, :] = y.astype(BF16)


def _inproj(x2, g, weights, rope_p, rope_mk, rope_mq):
    m = x2.shape[0]
    tm = TOKEN_TILE
    seq_tiles = SEQ // tm
    row = lambda i: (i, 0)
    by_batch = lambda i: (i // seq_tiles, 0, i % seq_tiles)
    seq_rows = lambda width: pl.BlockSpec((tm, width), lambda i: (i % seq_tiles, 0))
    seq_cols = lambda height: pl.BlockSpec((height, tm), lambda i: (0, i % seq_tiles))
    tables = ([seq_rows(BRANCH_WIDTH)] * 3 + [seq_cols(BRANCH_WIDTH)] * 3
              + [seq_rows(HEAD_PAD)] * 3 + [seq_cols(MLA_QK_DIM)] * 3)
    (g, *weights), whole_specs = _whole([g] + list(weights))
    return pl.pallas_call(
        _inproj_kernel,
        grid=(m // tm,),
        in_specs=[pl.BlockSpec((tm, D_MODEL), row)] + whole_specs + tables,
        out_specs=[pl.BlockSpec((tm, 4 * QK_PAD), row), pl.BlockSpec((1, 7 * BRANCH_WIDTH, tm), by_batch),
                   pl.BlockSpec((1, N_HEADS * MLA_QK_DIM, tm), by_batch), pl.BlockSpec((tm, HEAD_PAD), row)],
        out_shape=[jax.ShapeDtypeStruct((m, 4 * QK_PAD), BF16),
                   jax.ShapeDtypeStruct((m // SEQ, 7 * BRANCH_WIDTH, SEQ), BF16),
                   jax.ShapeDtypeStruct((m // SEQ, N_HEADS * MLA_QK_DIM, SEQ), BF16),
                   jax.ShapeDtypeStruct((m, HEAD_PAD), F32)],
        compiler_params=_params("parallel"),
        name="inproj",
    )(x2, g, *weights, *rope_p, *[t.T for t in rope_p], *rope_mk, *[t.T for t in rope_mq])


def _fox_bias_kernel(f_ref, bf_ref, to_k_ref, to_q_ref, q_ones_ref, k_ones_ref, qb_ref, kb_ref):
    chunk = 256
    ri = lax.broadcasted_iota(jnp.int32, (chunk, chunk), 0)
    ci = lax.broadcasted_iota(jnp.int32, (chunk, chunk), 1)
    tril = jnp.where(ri >= ci, 1.0, 0.0).astype(BF16)
    lane = lax.broadcasted_iota(jnp.int32, (chunk, HEAD_PAD), 1)
    carry = jnp.zeros((1, HEAD_PAD), F32)
    for c in range(SEQ // chunk):
        rows = slice(c * chunk, (c + 1) * chunk)
        z = f_ref[0, rows, :] + bf_ref[...]
        logf = jnp.minimum(z, 0.0) - jnp.log(1.0 + jnp.exp(-jnp.abs(z)))
        hi, mid, lo = _split3(logf)
        cum = _dot(tril, hi) + _dot(tril, mid) + _dot(tril, lo) + carry
        carry = cum[chunk - 1:chunk, :]
        c_hi, c_mid, c_lo = (p.astype(F32) for p in _split3(cum * LOG2E))
        packed = jnp.where(lane < N_HEADS, c_hi,
                           jnp.where(lane < 2 * N_HEADS, pltpu.roll(c_mid, N_HEADS, 1),
                                     pltpu.roll(c_lo, 2 * N_HEADS, 1))).astype(BF16)
        kb_ref[0, rows, :] = (k_ones_ref[...] - _dot(packed, to_k_ref[...])).astype(BF16)
        qb_ref[0, :, rows] = (_dot_nt(to_q_ref[...], packed) + q_ones_ref[...]).astype(BF16)


def _fox_bias(fpre3, bf, consts):
    b = fpre3.shape[0]
    (bf, *consts), whole_specs = _whole([bf] + list(consts))
    return pl.pallas_call(
        _fox_bias_kernel,
        grid=(b,),
        in_specs=[pl.BlockSpec((1, SEQ, HEAD_PAD), lambda i: (i, 0, 0))] + whole_specs,
        out_specs=[pl.BlockSpec((1, N_HEADS * BIAS_SLOTS, SEQ), lambda i: (i, 0, 0)),
                   pl.BlockSpec((1, SEQ, HEAD_PAD), lambda i: (i, 0, 0))],
        out_shape=[jax.ShapeDtypeStruct((b, N_HEADS * BIAS_SLOTS, SEQ), BF16),
                   jax.ShapeDtypeStruct((b, SEQ, HEAD_PAD), BF16)],
        compiler_params=_params("parallel"),
        name="fox_bias",
    )(fpre3, bf, *consts)


def _moba_bias_kernel(qt_ref, k_ref, ind_ref, hmask_ref, qb_ref):
    kmean = _dot(ind_ref[...], k_ref[0]) * (1.0 / MOBA_BLOCK) * hmask_ref[...]
    pieces = _split3(kmean)
    zeros = jnp.zeros((HEAD_PAD - HEAD_DIM, SEQ), BF16)
    qt_pad = jnp.concatenate(
        [part for h in range(N_HEADS) for part in (qt_ref[0, h * HEAD_DIM:(h + 1) * HEAD_DIM, :], zeros)], axis=0)
    gate = _dot(pieces[0], qt_pad) + _dot(pieces[1], qt_pad) + _dot(pieces[2], qt_pad)
    blk = lax.broadcasted_iota(jnp.int32, (N_MOBA_BLOCKS, SEQ), 0)
    own = lax.broadcasted_iota(jnp.int32, (N_MOBA_BLOCKS, SEQ), 1) // MOBA_BLOCK
    for h in range(N_HEADS):
        g = gate[h * N_MOBA_BLOCKS:(h + 1) * N_MOBA_BLOCKS, :]
        rank = jnp.zeros(g.shape, F32)
        for r in range(1, N_MOBA_BLOCKS):
            other = pltpu.roll(g, N_MOBA_BLOCKS - r, 0)
            other_blk = jnp.where(blk + r >= N_MOBA_BLOCKS, blk + r - N_MOBA_BLOCKS, blk + r)
            ahead = (other > g) | ((other == g) & (other_blk < blk))
            rank = rank + jnp.where(ahead & (other_blk < own), 1.0, 0.0)
        keep = ((blk < own) & (rank < MOBA_TOPK)) | (blk == own)
        bias = jnp.concatenate([jnp.where(keep, 0.0, NEG), jnp.zeros(g.shape, F32)], axis=0)
        qb_ref[0, h * BIAS_SLOTS:(h + 1) * BIAS_SLOTS, :] = bias.astype(BF16)


def _moba_bias(qvt, q_row, kpad3, k_col, ind, hmask):
    b = qvt.shape[0]
    return pl.pallas_call(
        _moba_bias_kernel,
        grid=(b,),
        in_specs=[pl.BlockSpec((1, BRANCH_WIDTH, SEQ), lambda i: (i, q_row, 0)),
                  pl.BlockSpec((1, SEQ, QK_PAD), lambda i: (i, 0, k_col)),
                  _full(ind.shape), _full(hmask.shape)],
        out_specs=pl.BlockSpec((1, N_HEADS * BIAS_SLOTS, SEQ), lambda i: (i, 0, 0)),
        out_shape=jax.ShapeDtypeStruct((b, N_HEADS * BIAS_SLOTS, SEQ), BF16),
        compiler_params=_params("parallel"),
        name="moba_bias",
    )(qvt, kpad3, ind, hmask)


def _attn_kernel(*refs, q_dim, biased, dilated):
    if biased:
        qt_ref, qb_ref, k_ref, kb_ref, vt_ref, tab_ref, o_ref, sa_ref, sb_ref, m_ref, acc_ref = refs
    else:
        qt_ref, k_ref, vt_ref, tab_ref, o_ref, sa_ref, sb_ref, m_ref, acc_ref = refs
    t = ATTN_TILE
    n_tiles = N_ATTN_TILES
    ones_rows = jnp.ones((ACC_ROWS - HEAD_DIM, t), BF16)

    m_ref[...] = jnp.full(m_ref.shape, NEG, F32)
    acc_ref[...] = jnp.zeros(acc_ref.shape, F32)

    def scores(i, j, s_ref):
        qo = pl.multiple_of(i * t, t)
        ko = pl.multiple_of(j * t, t)
        for h in range(N_HEADS):
            parts = [qt_ref[0, h * q_dim:(h + 1) * q_dim, pl.ds(qo, t)]]
            k_op = k_ref[0, pl.ds(ko, t), h * HEAD_PAD:(h + 1) * HEAD_PAD]
            if biased:
                k_op = k_op + kb_ref[0, pl.ds(ko, t), :]
                if h > 0:
                    parts.append(jnp.zeros((h * BIAS_SLOTS, t), BF16))
                parts.append(qb_ref[0, h * BIAS_SLOTS:(h + 1) * BIAS_SLOTS, pl.ds(qo, t)])
            fill = HEAD_PAD - sum(p.shape[0] for p in parts)
            if fill:
                parts.append(jnp.zeros((fill, t), BF16))
            s_ref[h] = _dot(k_op, jnp.concatenate(parts, axis=0))

    def update(i, j, s_ref, bias):
        qo = pl.multiple_of(i * t, t)
        ko = pl.multiple_of(j * t, t)
        for h in range(N_HEADS):
            s = s_ref[h] if bias is None else s_ref[h] + bias
            m_old = m_ref[h, :, pl.ds(qo, t)]
            m_new = jnp.maximum(m_old, jnp.max(s, axis=0, keepdims=True))
            m_ref[h, :, pl.ds(qo, t)] = m_new
            p = jnp.exp2(s - m_new).astype(BF16)
            v_aug = jnp.concatenate([vt_ref[0, h * HEAD_DIM:(h + 1) * HEAD_DIM, pl.ds(ko, t)], ones_rows], axis=0)
            rows = slice(h * ACC_ROWS, (h + 1) * ACC_ROWS)
            acc_ref[rows, pl.ds(qo, t)] = (acc_ref[rows, pl.ds(qo, t)] * jnp.exp2(m_old - m_new)
                                           + _dot(v_aug, p))

    def pipeline(start, succ, n_steps, per_trip, bias_of):
        def trip(_, ij):
            cur = ij
            for step in range(per_trip):
                s_cur, s_next = (sa_ref, sb_ref) if step % 2 == 0 else (sb_ref, sa_ref)
                nxt = succ(*cur)
                scores(*nxt, s_next)
                update(*cur, s_cur, bias_of(*cur))
                cur = nxt
            return cur

        assert n_steps % per_trip == 0 and per_trip % 2 == 0
        scores(*start, sa_ref)
        lax.fori_loop(0, n_steps // per_trip, trip, (jnp.int32(start[0]), jnp.int32(start[1])))

    def succ_diag(i, j):
        nxt = jnp.minimum(i + 1, n_tiles - 1)
        return nxt, nxt

    def succ_below(i, j):
        last = i == n_tiles - 1
        return (jnp.where(last, jnp.minimum(j + 2, n_tiles - 1), i + 1),
                jnp.where(last, jnp.minimum(j + 1, n_tiles - 2), j))

    pipeline((0, 0), succ_diag, n_tiles, DIAG_STEPS_PER_TRIP, lambda i, j: tab_ref[0])
    pipeline((1, 0), succ_below, n_tiles * (n_tiles - 1) // 2, BELOW_STEPS_PER_TRIP,
             (lambda i, j: tab_ref[i - j]) if dilated else (lambda i, j: None))

    for i in range(n_tiles):
        q_cols = slice(i * t, (i + 1) * t)
        heads = []
        for h in range(N_HEADS):
            denom = acc_ref[h * ACC_ROWS + HEAD_DIM:h * ACC_ROWS + HEAD_DIM + 1, q_cols]
            heads.append(acc_ref[h * ACC_ROWS:h * ACC_ROWS + HEAD_DIM, q_cols] * (1.0 / denom))
        o_ref[0, q_cols, :] = jnp.concatenate(heads, axis=0).T.astype(o_ref.dtype)


def _attention(qt, q_row, q_dim, k, k_col, vt, v_row, table, dilated=False, q_bias=None, k_bias=None):
    b = qt.shape[0]
    t = ATTN_TILE
    biased = q_bias is not None
    in_specs = [pl.BlockSpec((1, N_HEADS * q_dim, SEQ), lambda bi: (bi, q_row, 0))]
    args = [qt]
    if biased:
        in_specs.append(pl.BlockSpec((1, N_HEADS * BIAS_SLOTS, SEQ), lambda bi: (bi, 0, 0)))
        args.append(q_bias)
    in_specs.append(pl.BlockSpec((1, SEQ, QK_PAD), lambda bi: (bi, 0, k_col)))
    args.append(k)
    if biased:
        per_batch = k_bias.shape[0] > 1
        in_specs.append(pl.BlockSpec((1, SEQ, HEAD_PAD), lambda bi: (bi if per_batch else 0, 0, 0)))
        args.append(k_bias)
    in_specs += [pl.BlockSpec((1, BRANCH_WIDTH, SEQ), lambda bi: (bi, v_row, 0)), _full(table.shape)]
    args += [vt, table]
    return pl.pallas_call(
        functools.partial(_attn_kernel, q_dim=q_dim, biased=biased, dilated=dilated),
        grid=(b,),
        in_specs=in_specs,
        out_specs=pl.BlockSpec((1, SEQ, BRANCH_WIDTH), lambda bi: (bi, 0, 0)),
        out_shape=jax.ShapeDtypeStruct((b, SEQ, BRANCH_WIDTH), BF16),
        scratch_shapes=[pltpu.VMEM((N_HEADS, t, t), F32), pltpu.VMEM((N_HEADS, t, t), F32),
                        pltpu.VMEM((N_HEADS, 1, SEQ), F32), pltpu.VMEM((N_HEADS * ACC_ROWS, SEQ), F32)],
        compiler_params=_params("parallel"),
        name="attn_dil" if dilated else ("attn_bias" if biased else "attn"),
    )(*args)


def _merge_kernel(x_ref, gpre_ref, wg_ref, y0_ref, y1_ref, y2_ref, y3_ref, wb_ref, wo_ref, gpost_ref,
                  o_ref):
    x = x_ref[...]
    h = _rms(x, gpre_ref[...]).astype(BF16)
    mix = None
    for c in range(D_MODEL // MERGE_CHUNK):
        cols = slice(c * MERGE_CHUNK, (c + 1) * MERGE_CHUNK)
        merged = None
        for n, y_ref in enumerate((y0_ref, y1_ref, y2_ref, y3_ref)):
            gate = jax.nn.sigmoid(_dot(h, wg_ref[:, n * D_MODEL + c * MERGE_CHUNK:
                                                 n * D_MODEL + (c + 1) * MERGE_CHUNK]))
            term = gate * _dot(y_ref[...], wb_ref[n, :, cols])
            merged = term if merged is None else merged + term
        part = _dot(merged.astype(BF16), wo_ref[cols, :])
        mix = part if mix is None else mix + part
    o_ref[...] = x + _rms(mix, gpost_ref[...])


def _merge(x2, gpre, wg, ys, wb, wo, gpost):
    m = x2.shape[0]
    tm = TOKEN_TILE
    row = lambda i: (i, 0)
    yspec = pl.BlockSpec((tm, BRANCH_WIDTH), row)
    (gpre, wg, wb, wo, gpost), (s_gpre, s_wg, s_wb, s_wo, s_gpost) = _whole([gpre, wg, wb, wo, gpost])
    return pl.pallas_call(
        _merge_kernel,
        grid=(m // tm,),
        in_specs=[pl.BlockSpec((tm, D_MODEL), row), s_gpre, s_wg, yspec, yspec, yspec, yspec,
                  s_wb, s_wo, s_gpost],
        out_specs=pl.BlockSpec((tm, D_MODEL), row),
        out_shape=jax.ShapeDtypeStruct((m, D_MODEL), F32),
        compiler_params=_params("parallel"),
        name="merge",
    )(x2, gpre, wg, *ys, wb, wo, gpost)


def _mlp_kernel(x_ref, gpre_ref, wup_ref, wdown_ref, gpost_ref, o_ref):
    x = x_ref[...]
    h = _rms(x, gpre_ref[...]).astype(BF16)
    acc = None
    for c in range(D_FF // FF_CHUNK):
        cols = slice(c * FF_CHUNK, (c + 1) * FF_CHUNK)
        up = jnp.maximum(_dot(h, wup_ref[:, cols]), 0.0)
        term = _dot((up * up).astype(BF16), wdown_ref[cols, :])
        acc = term if acc is None else acc + term
    o_ref[...] = x + _rms(acc, gpost_ref[...])


def _mlp(x2, gpre, wup, wdown, gpost):
    m = x2.shape[0]
    tm = TOKEN_TILE
    row = lambda i: (i, 0)
    (gpre, wup, wdown, gpost), whole_specs = _whole([gpre, wup, wdown, gpost])
    return pl.pallas_call(
        _mlp_kernel,
        grid=(m // tm,),
        in_specs=[pl.BlockSpec((tm, D_MODEL), row)] + whole_specs,
        out_specs=pl.BlockSpec((tm, D_MODEL), row),
        out_shape=jax.ShapeDtypeStruct((m, D_MODEL), F32),
        compiler_params=_params("parallel"),
        name="mlp",
    )(x2, gpre, wup, wdown, gpost)


def kernel(x, w_in, b_forget, g_cq, g_ckv, w_uq, w_uk, w_uv, w_branch, w_out, w_up, w_down,
           g_pre_mix, g_post_mix, g_pre_mlp, g_post_mlp):
    b, s, d = x.shape
    assert (s, d) == (SEQ, D_MODEL)
    depth = w_in.shape[0]
    m = b * s
    rope_p, rope_mk, rope_mq = _rope_tables()
    fox_consts = _fox_constants()
    moba_ind, moba_hmask, moba_kb = _block_indicator(), _block_head_mask(), _moba_key_bias()
    dil_bias = _dilated_bias()
    causal_bias = _causal_bias()

    bw = BRANCH_WIDTH
    o_f = 3 * bw
    o_dil = o_f + N_HEADS
    o_moba = o_dil + 3 * bw
    o_cq = o_moba + 3 * bw
    o_ckv = o_cq + MLA_Q_RANK
    o_kr = o_ckv + MLA_KV_RANK
    o_gate = o_kr + MLA_ROPE_DIM

    qkv_at = (0, o_dil, o_moba)
    wk_all = jnp.concatenate([w_in[:, :, o + bw:o + 2 * bw] for o in qkv_at], axis=2).astype(BF16)
    wt_all = jnp.swapaxes(jnp.concatenate([w_in[:, :, o + 2 * bw:o + 3 * bw] for o in qkv_at]
                                          + [w_in[:, :, o:o + bw] for o in qkv_at], axis=2), 1, 2).astype(BF16)
    wsmall_all = jnp.concatenate(
        [w_in[:, :, o_cq:o_kr], jnp.zeros((depth, d, MLA_NOPE_DIM), F32), w_in[:, :, o_kr:o_gate],
         jnp.zeros((depth, d, HEAD_PAD - MLA_QK_DIM), F32)], axis=2).astype(BF16)
    wf_all = jnp.concatenate([jnp.swapaxes(w_in[:, :, o_f:o_dil], 1, 2),
                              jnp.zeros((depth, F32_SUBLANES - N_HEADS, d), F32)], axis=1)
    wgate_all = w_in[:, :, o_gate:].astype(BF16)
    bf_all = jnp.concatenate([b_forget, jnp.zeros((depth, HEAD_PAD - N_HEADS), F32)], axis=1)
    wuqt_all = jnp.swapaxes(w_uq, 1, 2).astype(BF16)
    wuk_all = jnp.concatenate(
        [part for h in range(N_HEADS) for part in (w_uk[:, :, h * MLA_NOPE_DIM:(h + 1) * MLA_NOPE_DIM],
                                                   jnp.zeros((depth, MLA_KV_RANK, HEAD_PAD - MLA_NOPE_DIM), F32))],
        axis=2).astype(BF16)
    wuvt_all = jnp.swapaxes(w_uv, 1, 2).astype(BF16)
    wb_all, wo_all = w_branch.astype(BF16), w_out.astype(BF16)
    wup_all, wdown_all = w_up.astype(BF16), w_down.astype(BF16)

    rows = lambda g: g.reshape(depth, 1, -1)
    g_cq, g_ckv, bf_all = rows(g_cq), rows(g_ckv), rows(bf_all)
    g_pre_mix, g_post_mix, g_pre_mlp, g_post_mlp = (rows(g) for g in (g_pre_mix, g_post_mix, g_pre_mlp,
                                                                       g_post_mlp))

    x2 = x.reshape(m, d)
    for l in range(depth):
        weights = [(w, l) for w in (wk_all, wt_all, wsmall_all, wf_all, g_cq, g_ckv, wuqt_all, wuk_all,
                                    wuvt_all)]
        kpad, qvt, lqt, fpre = _inproj(x2, (g_pre_mix, l), weights, rope_p, rope_mk, rope_mq)
        kpad3 = kpad.reshape(b, s, 4 * QK_PAD)
        fox_qb, fox_kb = _fox_bias(fpre.reshape(b, s, HEAD_PAD), (bf_all, l), fox_consts)
        moba_qb = _moba_bias(qvt, 5, kpad3, 2, moba_ind, moba_hmask)
        y_fox = _attention(qvt, 3, HEAD_DIM, kpad3, 0, qvt, 0, causal_bias, q_bias=fox_qb, k_bias=fox_kb)
        y_dil = _attention(qvt, 4, HEAD_DIM, kpad3, 1, qvt, 1, dil_bias, dilated=True)
        y_moba = _attention(qvt, 5, HEAD_DIM, kpad3, 2, qvt, 2, causal_bias, q_bias=moba_qb, k_bias=moba_kb)
        y_mla = _attention(lqt, 0, MLA_QK_DIM, kpad3, 3, qvt, 6, causal_bias)
        ys = [y.reshape(m, BRANCH_WIDTH) for y in (y_fox, y_dil, y_moba, y_mla)]
        x2 = _merge(x2, (g_pre_mix, l), (wgate_all, l), ys, (wb_all, l), (wo_all, l), (g_post_mix, l))
        x2 = _mlp(x2, (g_pre_mlp, l), (wup_all, l), (wdown_all, l), (g_post_mlp, l))
    return x2.reshape(b, s, d)
```

```python
import functools

import numpy as np
import jax
import jax.numpy as jnp
from jax import lax
from jax.experimental import pallas as pl
from jax.experimental.pallas import tpu as pltpu

LANES = 128
F32_SUBLANES = 8
BF16_SUBLANES = 16
V7X_VMEM_BYTES = 64 * 1024 * 1024

D_MODEL = 1024
SEQ = 2048
HEAD_DIM = 64
N_HEADS = 4
BRANCH_WIDTH = N_HEADS * HEAD_DIM
HEAD_PAD = LANES
QK_PAD = N_HEADS * HEAD_PAD
BIAS_SLOTS = BF16_SUBLANES
ROT_DIM = HEAD_DIM // 4
ROPE_THETA = 500000.0
DILATED_PAIRS = ((128, 1), (512, 4), (2048, 16))
MOBA_BLOCK = 256
MOBA_TOPK = 3
N_MOBA_BLOCKS = SEQ // MOBA_BLOCK
MLA_Q_RANK = 256
MLA_KV_RANK = 128
MLA_NOPE_DIM = 64
MLA_ROPE_DIM = 32
MLA_QK_DIM = MLA_NOPE_DIM + MLA_ROPE_DIM
D_FF = 4 * D_MODEL
NORM_EPS = 1e-6
NEG = -1e30
LOG2E = 1.4426950408889634

TOKEN_TILE = 512
ATTN_TILE = 256
N_ATTN_TILES = SEQ // ATTN_TILE
ACC_ROWS = HEAD_DIM + BF16_SUBLANES
DIAG_STEPS_PER_TRIP = N_ATTN_TILES
BELOW_STEPS_PER_TRIP = N_ATTN_TILES * (N_ATTN_TILES - 1) // 4
FF_CHUNK = 512
MERGE_CHUNK = 256
VMEM_LIMIT = V7X_VMEM_BYTES * 3 // 4

F32 = jnp.float32
BF16 = jnp.bfloat16


def _dot(a, b):
    return jnp.dot(a, b, preferred_element_type=F32)


def _dot_nt(a, b):
    return lax.dot_general(a, b, (((1,), (1,)), ((), ())), preferred_element_type=F32)


def _rms(x, g):
    return x * lax.rsqrt(jnp.mean(x * x, axis=-1, keepdims=True) + NORM_EPS) * g


def _rope(y, a, bm, bp, half, axis):
    n = y.shape[axis]
    return y * a + pltpu.roll(y, n - half, axis) * bm + pltpu.roll(y, half, axis) * bp


def _split3(x):
    hi = x.astype(BF16)
    r = x - hi.astype(F32)
    mid = r.astype(BF16)
    lo = (r - mid.astype(F32)).astype(BF16)
    return hi, mid, lo


def _rope_tables():
    def tables(dim):
        inv_freq = 1.0 / (ROPE_THETA ** (jnp.arange(0, dim, 2, dtype=F32) / dim))
        ang = jnp.arange(SEQ, dtype=F32)[:, None] * inv_freq[None, :]
        return jnp.cos(ang), jnp.sin(ang)

    def expand(cos, sin, width, period, start, half):
        pos = np.arange(width) % period - start
        first = (pos >= 0) & (pos < half)
        second = (pos >= half) & (pos < 2 * half)
        idx = np.where(first, pos, np.where(second, pos - half, 0))
        a = jnp.where((first | second)[None, :], cos[:, idx], 1.0)
        bm = jnp.where(first[None, :], -sin[:, idx], 0.0)
        bp = jnp.where(second[None, :], sin[:, idx], 0.0)
        return a, bm, bp

    cos_p, sin_p = tables(ROT_DIM)
    cos_m, sin_m = tables(MLA_ROPE_DIM)
    partial = expand(cos_p, sin_p, BRANCH_WIDTH, HEAD_DIM, 0, ROT_DIM // 2)
    mla_k = expand(cos_m, sin_m, HEAD_PAD, HEAD_PAD, MLA_NOPE_DIM, MLA_ROPE_DIM // 2)
    mla_q = expand(cos_m, sin_m, MLA_QK_DIM, MLA_QK_DIM, MLA_NOPE_DIM, MLA_ROPE_DIM // 2)
    return partial, mla_k, mla_q


def _fox_constants():
    to_k = np.zeros((HEAD_PAD, HEAD_PAD), np.float32)
    to_q = np.zeros((N_HEADS * BIAS_SLOTS, HEAD_PAD), np.float32)
    q_ones = np.zeros((N_HEADS * BIAS_SLOTS, 1), np.float32)
    k_ones = np.zeros((1, HEAD_PAD), np.float32)
    for h in range(N_HEADS):
        for j in range(3):
            to_k[4 * j + h, HEAD_DIM + BIAS_SLOTS * h + 3 + j] = 1.0
            to_q[BIAS_SLOTS * h + j, 4 * j + h] = 1.0
        q_ones[BIAS_SLOTS * h + 3:BIAS_SLOTS * h + 6] = 1.0
        k_ones[0, HEAD_DIM + BIAS_SLOTS * h:HEAD_DIM + BIAS_SLOTS * h + 3] = 1.0
    return [jnp.asarray(to_k, BF16), jnp.asarray(to_q, BF16), jnp.asarray(q_ones), jnp.asarray(k_ones)]


def _moba_key_bias():
    out = np.zeros((1, SEQ, HEAD_PAD), np.float32)
    blk = np.arange(SEQ) // MOBA_BLOCK
    for h in range(N_HEADS):
        for n in range(N_MOBA_BLOCKS):
            out[0, :, HEAD_DIM + BIAS_SLOTS * h + n] = blk == n
    return jnp.asarray(out, BF16)


def _block_indicator():
    ind = np.zeros((HEAD_PAD, SEQ), np.float32)
    for h in range(N_HEADS):
        for n in range(N_MOBA_BLOCKS):
            ind[N_MOBA_BLOCKS * h + n, n * MOBA_BLOCK:(n + 1) * MOBA_BLOCK] = 1.0
    return jnp.asarray(ind, BF16)


def _block_head_mask():
    m = np.zeros((HEAD_PAD, QK_PAD), np.float32)
    for h in range(N_HEADS):
        m[N_MOBA_BLOCKS * h:N_MOBA_BLOCKS * (h + 1), h * HEAD_PAD:h * HEAD_PAD + HEAD_DIM] = 1.0
    return jnp.asarray(m)


def _causal_bias():
    key = np.arange(ATTN_TILE)[:, None]
    query = np.arange(ATTN_TILE)[None, :]
    return jnp.asarray(np.where(key <= query, 0.0, NEG).astype(np.float32)[None])


def _dilated_bias():
    r = np.arange(ATTN_TILE)[None, :]
    c = np.arange(ATTN_TILE)[:, None]
    count = np.zeros((SEQ // ATTN_TILE, ATTN_TILE, ATTN_TILE), np.float32)
    for delta in range(SEQ // ATTN_TILE):
        d = delta * ATTN_TILE + r - c
        for window, dil in DILATED_PAIRS:
            count[delta] += (d >= 0) & (d <= window) & (d % dil == 0)
    return jnp.asarray(np.where(count > 0, np.log2(np.maximum(count, 1.0)), NEG).astype(np.float32))


def _full(shape):
    return pl.BlockSpec(shape, lambda *_: (0,) * len(shape), pipeline_mode=pl.Buffered(1))


def _whole(operands):
    arrays, specs = [], []
    for op in operands:
        if isinstance(op, tuple):
            stacked, layer = op
            index = (layer,) + (0,) * (stacked.ndim - 1)
            arrays.append(stacked)
            specs.append(pl.BlockSpec((None,) + stacked.shape[1:], lambda *_, index=index: index,
                                      pipeline_mode=pl.Buffered(1)))
        else:
            arrays.append(op)
            specs.append(_full(op.shape))
    return arrays, specs


def _params(*sem):
    return pltpu.CompilerParams(dimension_semantics=sem, vmem_limit_bytes=VMEM_LIMIT)


def _inproj_kernel(x_ref, g_ref, wk_ref, wt_ref, wsmall_ref, wf_ref, gq_ref, gkv_ref, wuqt_ref, wuk_ref,
                   wuvt_ref, ra_ref, rbm_ref, rbp_ref, rat_ref, rbmt_ref, rbpt_ref, mra_ref, mrbm_ref,
                   mrbp_ref, mrat_ref, mrbmt_ref, mrbpt_ref, kpad_ref, qvt_ref, lqt_ref, f_ref):
    hf = _rms(x_ref[...], g_ref[...])
    h = hf.astype(BF16)
    tm = h.shape[0]

    lane = lax.broadcasted_iota(jnp.int32, f_ref.shape, 1)
    f = jnp.zeros(f_ref.shape, F32)
    for j in range(N_HEADS):
        col = jnp.sum(hf * wf_ref[j:j + 1, :], axis=-1, keepdims=True)
        f = jnp.where(lane == j, col, f)
    f_ref[...] = f

    small = _dot(h, wsmall_ref[...])
    half = MLA_ROPE_DIM // 2
    cq = _rms(small[:, 0:MLA_Q_RANK], gq_ref[...]).astype(BF16)
    ckv = _rms(small[:, MLA_Q_RANK:MLA_Q_RANK + MLA_KV_RANK], gkv_ref[...]).astype(BF16)
    kr = _rope(small[:, MLA_Q_RANK + MLA_KV_RANK:], mra_ref[...], mrbm_ref[...], mrbp_ref[...], half, 1)
    qf_t = _dot_nt(wuqt_ref[...], cq)
    kf = _dot(ckv, wuk_ref[...])
    for hd in range(N_HEADS):
        rows = slice(hd * MLA_QK_DIM, (hd + 1) * MLA_QK_DIM)
        q_h = _rope(qf_t[rows, :], mrat_ref[...], mrbmt_ref[...], mrbpt_ref[...], half, 0)
        lqt_ref[0, rows, :] = (q_h * (MLA_QK_DIM ** -0.5 * LOG2E)).astype(BF16)
        col = 3 * QK_PAD + hd * HEAD_PAD
        kpad_ref[:, col:col + HEAD_PAD] = (kf[:, hd * HEAD_PAD:(hd + 1) * HEAD_PAD] + kr).astype(BF16)
    qvt_ref[0, 6 * BRANCH_WIDTH:, :] = _dot_nt(wuvt_ref[...], ckv).astype(BF16)

    low_half = lax.broadcasted_iota(jnp.int32, (tm, HEAD_PAD), 1) < HEAD_DIM
    for c in range(3):
        y = _dot(h, wk_ref[:, c * BRANCH_WIDTH:(c + 1) * BRANCH_WIDTH])
        if c >= 1:
            y = _rope(y, ra_ref[...], rbm_ref[...], rbp_ref[...], ROT_DIM // 2, 1)
        for pair in range(2):
            both = y[:, pair * HEAD_PAD:(pair + 1) * HEAD_PAD]
            for odd, src in enumerate((both, pltpu.roll(both, HEAD_DIM, 1))):
                col = c * QK_PAD + (2 * pair + odd) * HEAD_PAD
                kpad_ref[:, col:col + HEAD_PAD] = jnp.where(low_half, src, 0.0).astype(BF16)

    t_all = _dot_nt(wt_ref[...], h)
    for c in range(6):
        rows = slice(c * BRANCH_WIDTH, (c + 1) * BRANCH_WIDTH)
        y = t_all[rows, :]
        if c >= 4:
            y = _rope(y, rat_ref[...], rbmt_ref[...], rbpt_ref[...], ROT_DIM // 2, 0)
        if c >= 3:
            y = y * (HEAD_DIM ** -0.5 * LOG2E)
        qvt_ref[0, rows, :] = y.astype(BF16)


def _inproj(x2, g, weights, rope_p, rope_mk, rope_mq):
    m = x2.shape[0]
    tm = TOKEN_TILE
    seq_tiles = SEQ // tm
    row = lambda i: (i, 0)
    by_batch = lambda i: (i // seq_tiles, 0, i % seq_tiles)
    seq_rows = lambda width: pl.BlockSpec((tm, width), lambda i: (i % seq_tiles, 0))
    seq_cols = lambda height: pl.BlockSpec((height, tm), lambda i: (0, i % seq_tiles))
    tables = ([seq_rows(BRANCH_WIDTH)] * 3 + [seq_cols(BRANCH_WIDTH)] * 3
              + [seq_rows(HEAD_PAD)] * 3 + [seq_cols(MLA_QK_DIM)] * 3)
    (g, *weights), whole_specs = _whole([g] + list(weights))
    return pl.pallas_call(
        _inproj_kernel,
        grid=(m // tm,),
        in_specs=[pl.BlockSpec((tm, D_MODEL), row)] + whole_specs + tables,
        out_specs=[pl.BlockSpec((tm, 4 * QK_PAD), row), pl.BlockSpec((1, 7 * BRANCH_WIDTH, tm), by_batch),
                   pl.BlockSpec((1, N_HEADS * MLA_QK_DIM, tm), by_batch), pl.BlockSpec((tm, HEAD_PAD), row)],
        out_shape=[jax.ShapeDtypeStruct((m, 4 * QK_PAD), BF16),
                   jax.ShapeDtypeStruct((m // SEQ, 7 * BRANCH_WIDTH, SEQ), BF16),
                   jax.ShapeDtypeStruct((m // SEQ, N_HEADS * MLA_QK_DIM, SEQ), BF16),
                   jax.ShapeDtypeStruct((m, HEAD_PAD), F32)],
        compiler_params=_params("parallel"),
        name="inproj",
    )(x2, g, *weights, *rope_p, *[t.T for t in rope_p], *rope_mk, *[t.T for t in rope_mq])


def _fox_bias_kernel(f_ref, bf_ref, to_k_ref, to_q_ref, q_ones_ref, k_ones_ref, qb_ref, kb_ref):
    chunk = 256
    ri = lax.broadcasted_iota(jnp.int32, (chunk, chunk), 0)
    ci = lax.broadcasted_iota(jnp.int32, (chunk, chunk), 1)
    tril = jnp.where(ri >= ci, 1.0, 0.0).astype(BF16)
    lane = lax.broadcasted_iota(jnp.int32, (chunk, HEAD_PAD), 1)
    carry = jnp.zeros((1, HEAD_PAD), F32)
    for c in range(SEQ // chunk):
        rows = slice(c * chunk, (c + 1) * chunk)
        z = f_ref[0, rows, :] + bf_ref[...]
        logf = jnp.minimum(z, 0.0) - jnp.log(1.0 + jnp.exp(-jnp.abs(z)))
        hi, mid, lo = _split3(logf)
        cum = _dot(tril, hi) + _dot(tril, mid) + _dot(tril, lo) + carry
        carry = cum[chunk - 1:chunk, :]
        c_hi, c_mid, c_lo = (p.astype(F32) for p in _split3(cum * LOG2E))
        packed = jnp.where(lane < N_HEADS, c_hi,
                           jnp.where(lane < 2 * N_HEADS, pltpu.roll(c_mid, N_HEADS, 1),
                                     pltpu.roll(c_lo, 2 * N_HEADS, 1))).astype(BF16)
        kb_ref[0, rows, :] = (k_ones_ref[...] - _dot(packed, to_k_ref[...])).astype(BF16)
        qb_ref[0, :, rows] = (_dot_nt(to_q_ref[...], packed) + q_ones_ref[...]).astype(BF16)


def _fox_bias(fpre3, bf, consts):
    b = fpre3.shape[0]
    (bf, *consts), whole_specs = _whole([bf] + list(consts))
    return pl.pallas_call(
        _fox_bias_kernel,
        grid=(b,),
        in_specs=[pl.BlockSpec((1, SEQ, HEAD_PAD), lambda i: (i, 0, 0))] + whole_specs,
        out_specs=[pl.BlockSpec((1, N_HEADS * BIAS_SLOTS, SEQ), lambda i: (i, 0, 0)),
                   pl.BlockSpec((1, SEQ, HEAD_PAD), lambda i: (i, 0, 0))],
        out_shape=[jax.ShapeDtypeStruct((b, N_HEADS * BIAS_SLOTS, SEQ), BF16),
                   jax.ShapeDtypeStruct((b, SEQ, HEAD_PAD), BF16)],
        compiler_params=_params("parallel"),
        name="fox_bias",
    )(fpre3, bf, *consts)


def _moba_bias_kernel(qt_ref, k_ref, ind_ref, hmask_ref, qb_ref):
    kmean = _dot(ind_ref[...], k_ref[0]) * (1.0 / MOBA_BLOCK) * hmask_ref[...]
    pieces = _split3(kmean)
    zeros = jnp.zeros((HEAD_PAD - HEAD_DIM, SEQ), BF16)
    qt_pad = jnp.concatenate(
        [part for h in range(N_HEADS) for part in (qt_ref[0, h * HEAD_DIM:(h + 1) * HEAD_DIM, :], zeros)], axis=0)
    gate = _dot(pieces[0], qt_pad) + _dot(pieces[1], qt_pad) + _dot(pieces[2], qt_pad)
    blk = lax.broadcasted_iota(jnp.int32, (N_MOBA_BLOCKS, SEQ), 0)
    own = lax.broadcasted_iota(jnp.int32, (N_MOBA_BLOCKS, SEQ), 1) // MOBA_BLOCK
    for h in range(N_HEADS):
        g = gate[h * N_MOBA_BLOCKS:(h + 1) * N_MOBA_BLOCKS, :]
        rank = jnp.zeros(g.shape, F32)
        for r in range(1, N_MOBA_BLOCKS):
            other = pltpu.roll(g, N_MOBA_BLOCKS - r, 0)
            other_blk = jnp.where(blk + r >= N_MOBA_BLOCKS, blk + r - N_MOBA_BLOCKS, blk + r)
            ahead = (other > g) | ((other == g) & (other_blk < blk))
            rank = rank + jnp.where(ahead & (other_blk < own), 1.0, 0.0)
        keep = ((blk < own) & (rank < MOBA_TOPK)) | (blk == own)
        bias = jnp.concatenate([jnp.where(keep, 0.0, NEG), jnp.zeros(g.shape, F32)], axis=0)
        qb_ref[0, h * BIAS_SLOTS:(h + 1) * BIAS_SLOTS, :] = bias.astype(BF16)


def _moba_bias(qvt, q_row, kpad3, k_col, ind, hmask):
    b = qvt.shape[0]
    return pl.pallas_call(
        _moba_bias_kernel,
        grid=(b,),
        in_specs=[pl.BlockSpec((1, BRANCH_WIDTH, SEQ), lambda i: (i, q_row, 0)),
                  pl.BlockSpec((1, SEQ, QK_PAD), lambda i: (i, 0, k_col)),
                  _full(ind.shape), _full(hmask.shape)],
        out_specs=pl.BlockSpec((1, N_HEADS * BIAS_SLOTS, SEQ), lambda i: (i, 0, 0)),
        out_shape=jax.ShapeDtypeStruct((b, N_HEADS * BIAS_SLOTS, SEQ), BF16),
        compiler_params=_params("parallel"),
        name="moba_bias",
    )(qvt, kpad3, ind, hmask)


def _attn_kernel(*refs, q_dim, biased, dilated):
    if biased:
        qt_ref, qb_ref, k_ref, kb_ref, vt_ref, tab_ref, o_ref, sa_ref, sb_ref, m_ref, acc_ref = refs
    else:
        qt_ref, k_ref, vt_ref, tab_ref, o_ref, sa_ref, sb_ref, m_ref, acc_ref = refs
    t = ATTN_TILE
    n_tiles = N_ATTN_TILES
    ones_rows = jnp.ones((ACC_ROWS - HEAD_DIM, t), BF16)

    m_ref[...] = jnp.full(m_ref.shape, NEG, F32)
    acc_ref[...] = jnp.zeros(acc_ref.shape, F32)

    def scores(i, j, s_ref):
        qo = pl.multiple_of(i * t, t)
        ko = pl.multiple_of(j * t, t)
        for h in range(N_HEADS):
            parts = [qt_ref[0, h * q_dim:(h + 1) * q_dim, pl.ds(qo, t)]]
            k_op = k_ref[0, pl.ds(ko, t), h * HEAD_PAD:(h + 1) * HEAD_PAD]
            if biased:
                k_op = k_op + kb_ref[0, pl.ds(ko, t), :]
                if h > 0:
                    parts.append(jnp.zeros((h * BIAS_SLOTS, t), BF16))
                parts.append(qb_ref[0, h * BIAS_SLOTS:(h + 1) * BIAS_SLOTS, pl.ds(qo, t)])
            fill = HEAD_PAD - sum(p.shape[0] for p in parts)
            if fill:
                parts.append(jnp.zeros((fill, t), BF16))
            s_ref[h] = _dot(k_op, jnp.concatenate(parts, axis=0))

    def update(i, j, s_ref, bias):
        qo = pl.multiple_of(i * t, t)
        ko = pl.multiple_of(j * t, t)
        for h in range(N_HEADS):
            s = s_ref[h] if bias is None else s_ref[h] + bias
            m_old = m_ref[h, :, pl.ds(qo, t)]
            m_new = jnp.maximum(m_old, jnp.max(s, axis=0, keepdims=True))
            m_ref[h, :, pl.ds(qo, t)] = m_new
            p = jnp.exp2(s - m_new).astype(BF16)
            v_aug = jnp.concatenate([vt_ref[0, h * HEAD_DIM:(h + 1) * HEAD_DIM, pl.ds(ko, t)], ones_rows], axis=0)
            rows = slice(h * ACC_ROWS, (h + 1) * ACC_ROWS)
            acc_ref[rows, pl.ds(qo, t)] = (acc_ref[rows, pl.ds(qo, t)] * jnp.exp2(m_old - m_new)
                                           + _dot(v_aug, p))

    def pipeline(start, succ, n_steps, per_trip, bias_of, scores_ready):
        def trip(_, ij):
            cur = ij
            for step in range(per_trip):
                s_cur, s_next = (sa_ref, sb_ref) if step % 2 == 0 else (sb_ref, sa_ref)
                nxt = succ(*cur)
                scores(*nxt, s_next)
                update(*cur, s_cur, bias_of(*cur))
                cur = nxt
            return cur

        assert n_steps % per_trip == 0 and per_trip % 2 == 0
        if not scores_ready:
            scores(*start, sa_ref)
        lax.fori_loop(0, n_steps // per_trip, trip, (jnp.int32(start[0]), jnp.int32(start[1])))

    below_start = (1, 0)

    def succ_diag(i, j):
        last = i == n_tiles - 1
        return jnp.where(last, below_start[0], i + 1), jnp.where(last, below_start[1], i + 1)

    def succ_below(i, j):
        last = j == i - 1
        return jnp.where(last, jnp.minimum(i + 1, n_tiles - 1), i), jnp.where(last, 0, j + 1)

    pipeline((0, 0), succ_diag, n_tiles, DIAG_STEPS_PER_TRIP, lambda i, j: tab_ref[0], scores_ready=False)
    pipeline(below_start, succ_below, n_tiles * (n_tiles - 1) // 2, BELOW_STEPS_PER_TRIP,
             (lambda i, j: tab_ref[i - j]) if dilated else (lambda i, j: None), scores_ready=True)

    for i in range(n_tiles):
        q_cols = slice(i * t, (i + 1) * t)
        heads = []
        for h in range(N_HEADS):
            denom = acc_ref[h * ACC_ROWS + HEAD_DIM:h * ACC_ROWS + HEAD_DIM + 1, q_cols]
            heads.append(acc_ref[h * ACC_ROWS:h * ACC_ROWS + HEAD_DIM, q_cols] * (1.0 / denom))
        o_ref[0, q_cols, :] = jnp.concatenate(heads, axis=0).T.astype(o_ref.dtype)


def _attention(qt, q_row, q_dim, k, k_col, vt, v_row, table, dilated=False, q_bias=None, k_bias=None):
    b = qt.shape[0]
    t = ATTN_TILE
    biased = q_bias is not None
    in_specs = [pl.BlockSpec((1, N_HEADS * q_dim, SEQ), lambda bi: (bi, q_row, 0))]
    args = [qt]
    if biased:
        in_specs.append(pl.BlockSpec((1, N_HEADS * BIAS_SLOTS, SEQ), lambda bi: (bi, 0, 0)))
        args.append(q_bias)
    in_specs.append(pl.BlockSpec((1, SEQ, QK_PAD), lambda bi: (bi, 0, k_col)))
    args.append(k)
    if biased:
        per_batch = k_bias.shape[0] > 1
        in_specs.append(pl.BlockSpec((1, SEQ, HEAD_PAD), lambda bi: (bi if per_batch else 0, 0, 0)))
        args.append(k_bias)
    in_specs += [pl.BlockSpec((1, BRANCH_WIDTH, SEQ), lambda bi: (bi, v_row, 0)), _full(table.shape)]
    args += [vt, table]
    return pl.pallas_call(
        functools.partial(_attn_kernel, q_dim=q_dim, biased=biased, dilated=dilated),
        grid=(b,),
        in_specs=in_specs,
        out_specs=pl.BlockSpec((1, SEQ, BRANCH_WIDTH), lambda bi: (bi, 0, 0)),
        out_shape=jax.ShapeDtypeStruct((b, SEQ, BRANCH_WIDTH), BF16),
        scratch_shapes=[pltpu.VMEM((N_HEADS, t, t), F32), pltpu.VMEM((N_HEADS, t, t), F32),
                        pltpu.VMEM((N_HEADS, 1, SEQ), F32), pltpu.VMEM((N_HEADS * ACC_ROWS, SEQ), F32)],
        compiler_params=_params("parallel"),
        name="attn_dil" if dilated else ("attn_bias" if biased else "attn"),
    )(*args)


def _merge_kernel(x_ref, gpre_ref, wg_ref, y0_ref, y1_ref, y2_ref, y3_ref, wb_ref, wo_ref, gpost_ref,
                  o_ref):
    x = x_ref[...]
    h = _rms(x, gpre_ref[...]).astype(BF16)
    mix = None
    for c in range(D_MODEL // MERGE_CHUNK):
        cols = slice(c * MERGE_CHUNK, (c + 1) * MERGE_CHUNK)
        merged = None
        for n, y_ref in enumerate((y0_ref, y1_ref, y2_ref, y3_ref)):
            gate = jax.nn.sigmoid(_dot(h, wg_ref[:, n * D_MODEL + c * MERGE_CHUNK:
                                                 n * D_MODEL + (c + 1) * MERGE_CHUNK]))
            term = gate * _dot(y_ref[...], wb_ref[n, :, cols])
            merged = term if merged is None else merged + term
        part = _dot(merged.astype(BF16), wo_ref[cols, :])
        mix = part if mix is None else mix + part
    o_ref[...] = x + _rms(mix, gpost_ref[...])


def _merge(x2, gpre, wg, ys, wb, wo, gpost):
    m = x2.shape[0]
    tm = TOKEN_TILE
    row = lambda i: (i, 0)
    yspec = pl.BlockSpec((tm, BRANCH_WIDTH), row)
    (gpre, wg, wb, wo, gpost), (s_gpre, s_wg, s_wb, s_wo, s_gpost) = _whole([gpre, wg, wb, wo, gpost])
    return pl.pallas_call(
        _merge_kernel,
        grid=(m // tm,),
        in_specs=[pl.BlockSpec((tm, D_MODEL), row), s_gpre, s_wg, yspec, yspec, yspec, yspec,
                  s_wb, s_wo, s_gpost],
        out_specs=pl.BlockSpec((tm, D_MODEL), row),
        out_shape=jax.ShapeDtypeStruct((m, D_MODEL), F32),
        compiler_params=_params("parallel"),
        name="merge",
    )(x2, gpre, wg, *ys, wb, wo, gpost)


def _mlp_kernel(x_ref, gpre_ref, wup_ref, wdown_ref, gpost_ref, o_ref):
    x = x_ref[...]
    h = _rms(x, gpre_ref[...]).astype(BF16)
    acc = None
    for c in range(D_FF // FF_CHUNK):
        cols = slice(c * FF_CHUNK, (c + 1) * FF_CHUNK)
        up = jnp.maximum(_dot(h, wup_ref[:, cols]), 0.0)
        term = _dot((up * up).astype(BF16), wdown_ref[cols, :])
        acc = term if acc is None else acc + term
    o_ref[...] = x + _rms(acc, gpost_ref[...])


def _mlp(x2, gpre, wup, wdown, gpost):
    m = x2.shape[0]
    tm = TOKEN_TILE
    row = lambda i: (i, 0)
    (gpre, wup, wdown, gpost), whole_specs = _whole([gpre, wup, wdown, gpost])
    return pl.pallas_call(
        _mlp_kernel,
        grid=(m // tm,),
        in_specs=[pl.BlockSpec((tm, D_MODEL), row)] + whole_specs,
        out_specs=pl.BlockSpec((tm, D_MODEL), row),
        out_shape=jax.ShapeDtypeStruct((m, D_MODEL), F32),
        compiler_params=_params("parallel"),
        name="mlp",
    )(x2, gpre, wup, wdown, gpost)


def kernel(x, w_in, b_forget, g_cq, g_ckv, w_uq, w_uk, w_uv, w_branch, w_out, w_up, w_down,
           g_pre_mix, g_post_mix, g_pre_mlp, g_post_mlp):
    b, s, d = x.shape
    assert (s, d) == (SEQ, D_MODEL)
    depth = w_in.shape[0]
    m = b * s
    rope_p, rope_mk, rope_mq = _rope_tables()
    fox_consts = _fox_constants()
    moba_ind, moba_hmask, moba_kb = _block_indicator(), _block_head_mask(), _moba_key_bias()
    dil_bias = _dilated_bias()
    causal_bias = _causal_bias()

    bw = BRANCH_WIDTH
    o_f = 3 * bw
    o_dil = o_f + N_HEADS
    o_moba = o_dil + 3 * bw
    o_cq = o_moba + 3 * bw
    o_ckv = o_cq + MLA_Q_RANK
    o_kr = o_ckv + MLA_KV_RANK
    o_gate = o_kr + MLA_ROPE_DIM

    qkv_at = (0, o_dil, o_moba)
    wk_all = jnp.concatenate([w_in[:, :, o + bw:o + 2 * bw] for o in qkv_at], axis=2).astype(BF16)
    wt_all = jnp.swapaxes(jnp.concatenate([w_in[:, :, o + 2 * bw:o + 3 * bw] for o in qkv_at]
                                          + [w_in[:, :, o:o + bw] for o in qkv_at], axis=2), 1, 2).astype(BF16)
    wsmall_all = jnp.concatenate(
        [w_in[:, :, o_cq:o_kr], jnp.zeros((depth, d, MLA_NOPE_DIM), F32), w_in[:, :, o_kr:o_gate],
         jnp.zeros((depth, d, HEAD_PAD - MLA_QK_DIM), F32)], axis=2).astype(BF16)
    wf_all = jnp.concatenate([jnp.swapaxes(w_in[:, :, o_f:o_dil], 1, 2),
                              jnp.zeros((depth, F32_SUBLANES - N_HEADS, d), F32)], axis=1)
    wgate_all = w_in[:, :, o_gate:].astype(BF16)
    bf_all = jnp.concatenate([b_forget, jnp.zeros((depth, HEAD_PAD - N_HEADS), F32)], axis=1)
    wuqt_all = jnp.swapaxes(w_uq, 1, 2).astype(BF16)
    wuk_all = jnp.concatenate(
        [part for h in range(N_HEADS) for part in (w_uk[:, :, h * MLA_NOPE_DIM:(h + 1) * MLA_NOPE_DIM],
                                                   jnp.zeros((depth, MLA_KV_RANK, HEAD_PAD - MLA_NOPE_DIM), F32))],
        axis=2).astype(BF16)
    wuvt_all = jnp.swapaxes(w_uv, 1, 2).astype(BF16)
    wb_all, wo_all = w_branch.astype(BF16), w_out.astype(BF16)
    wup_all, wdown_all = w_up.astype(BF16), w_down.astype(BF16)

    rows = lambda g: g.reshape(depth, 1, -1)
    g_cq, g_ckv, bf_all = rows(g_cq), rows(g_ckv), rows(bf_all)
    g_pre_mix, g_post_mix, g_pre_mlp, g_post_mlp = (rows(g) for g in (g_pre_mix, g_post_mix, g_pre_mlp,
                                                                       g_post_mlp))

    x2 = x.reshape(m, d)
    for l in range(depth):
        weights = [(w, l) for w in (wk_all, wt_all, wsmall_all, wf_all, g_cq, g_ckv, wuqt_all, wuk_all,
                                    wuvt_all)]
        kpad, qvt, lqt, fpre = _inproj(x2, (g_pre_mix, l), weights, rope_p, rope_mk, rope_mq)
        kpad3 = kpad.reshape(b, s, 4 * QK_PAD)
        fox_qb, fox_kb = _fox_bias(fpre.reshape(b, s, HEAD_PAD), (bf_all, l), fox_consts)
        moba_qb = _moba_bias(qvt, 5, kpad3, 2, moba_ind, moba_hmask)
        y_fox = _attention(qvt, 3, HEAD_DIM, kpad3, 0, qvt, 0, causal_bias, q_bias=fox_qb, k_bias=fox_kb)
        y_dil = _attention(qvt, 4, HEAD_DIM, kpad3, 1, qvt, 1, dil_bias, dilated=True)
        y_moba = _attention(qvt, 5, HEAD_DIM, kpad3, 2, qvt, 2, causal_bias, q_bias=moba_qb, k_bias=moba_kb)
        y_mla = _attention(lqt, 0, MLA_QK_DIM, kpad3, 3, qvt, 6, causal_bias)
        ys = [y.reshape(m, BRANCH_WIDTH) for y in (y_fox, y_dil, y_moba, y_mla)]
        x2 = _merge(x2, (g_pre_mix, l), (wgate_all, l), ys, (wb_all, l), (wo_all, l), (g_post_mix, l))
        x2 = _mlp(x2, (g_pre_mlp, l), (wup_all, l), (wdown_all, l), (g_post_mlp, l))
    return x2.reshape(b, s, d)
```
